```python
import jax, jax.numpy as jnp
from jax import lax
import numpy as np

D_MODEL = 1024
BATCH = 32
SEQ = 2048
DEPTH = 1
DEC_BATCH = 16
DEC_SEQ = 2048
PAST_LEN = 128

GRID_W = 64
HEAD_DIM = 64
N_Q_HEADS = 8
N_KV_HEADS = 2
Q_GROUP = N_Q_HEADS // N_KV_HEADS
ATTN_W = N_Q_HEADS * HEAD_DIM
KV_W = N_KV_HEADS * HEAD_DIM
LRU_W = D_MODEL - ATTN_W
LRU_BLOCKS = 8
LRU_BLOCK_DIM = LRU_W // LRU_BLOCKS
CONV_W = 4
LRU_C = 8.0
N_EXPERTS = 16
EXPERT_CAPACITY_FACTOR = 2
D_EXPERT = D_MODEL
Q_BLOCK = 128
ROPE_THETA = 10000.0
EPS = 1e-6
IN_W = ATTN_W + 2 * KV_W + 2 * LRU_W

kernel_name = "hybrid_gqa_rglru_ec_moe_encoder"


def _rmsnorm(x, g):
    x32 = x.astype(jnp.float32)
    y = x32 * lax.rsqrt(jnp.mean(x32 * x32, axis=-1, keepdims=True) + EPS)
    return (y * g.astype(jnp.float32)).astype(x.dtype)


def _axial_angles(seq_len):
    rows = seq_len // GRID_W
    row = jnp.repeat(jnp.arange(rows, dtype=jnp.float32), GRID_W)
    col = jnp.tile(jnp.arange(GRID_W, dtype=jnp.float32), rows)
    axis_dim = HEAD_DIM // 2
    inv_freq = ROPE_THETA ** (-jnp.arange(0, axis_dim, 2, dtype=jnp.float32) / axis_dim)
    return row[:, None] * inv_freq[None, :], col[:, None] * inv_freq[None, :]


def _rope_half(x, ang):
    f = ang.shape[-1]
    cos = jnp.cos(ang)[None, :, None, :]
    sin = jnp.sin(ang)[None, :, None, :]
    x1, x2 = x[..., :f], x[..., f:]
    return jnp.concatenate([x1 * cos - x2 * sin, x2 * cos + x1 * sin], axis=-1)


def _apply_axial_rope(x, row_ang, col_ang):
    half = HEAD_DIM // 2
    x32 = x.astype(jnp.float32)
    out = jnp.concatenate([_rope_half(x32[..., :half], row_ang),
                           _rope_half(x32[..., half:], col_ang)], axis=-1)
    return out.astype(x.dtype)


def _attention(q, k, v):
    B, S = q.shape[0], q.shape[1]
    nblk = S // Q_BLOCK
    qb = q.reshape(B, nblk, Q_BLOCK, N_KV_HEADS, Q_GROUP, HEAD_DIM)
    qb = jnp.moveaxis(qb, 1, 0)
    scale = HEAD_DIM ** -0.5

    def one_block(qblk):
        s = jnp.einsum('bqkgd,bskd->bkgqs', qblk, k).astype(jnp.float32) * scale
        p = jax.nn.softmax(s, axis=-1).astype(v.dtype)
        return jnp.einsum('bkgqs,bskd->bqkgd', p, v)

    o = lax.map(one_block, qb)
    return jnp.moveaxis(o, 0, 1).reshape(B, S, ATTN_W)


def _centred_dwconv(x, w, b):
    S = x.shape[1]
    left = CONV_W // 2
    right = CONV_W - 1 - left
    xp = jnp.pad(x, ((0, 0), (left, right), (0, 0)))
    out = b + xp[:, 0:S] * w[0]
    for tap in range(1, CONV_W):
        out = out + xp[:, tap:tap + S] * w[tap]
    return out


def _blockdiag(x, w, b):
    B, S = x.shape[0], x.shape[1]
    xb = x.reshape(B, S, LRU_BLOCKS, LRU_BLOCK_DIM)
    return jnp.einsum('bsnd,nde->bsne', xb, w).reshape(B, S, LRU_W) + b


def _lin_combine(c1, c2):
    a1, b1 = c1
    a2, b2 = c2
    return a1 * a2, a2 * b1 + b2


def _rglru_direction(xc, w_r, b_r, w_i, b_i, lam, reverse):
    r = jax.nn.sigmoid(_blockdiag(xc, w_r, b_r).astype(jnp.float32))
    i = jax.nn.sigmoid(_blockdiag(xc, w_i, b_i).astype(jnp.float32))
    log_a = -LRU_C * r * jax.nn.softplus(-lam.astype(jnp.float32))
    a = jnp.exp(log_a)
    mult = jnp.sqrt(-jnp.expm1(2.0 * log_a))
    bterm = mult * i * xc.astype(jnp.float32)
    _, h = lax.associative_scan(_lin_combine, (a, bterm), reverse=reverse, axis=1)
    return h


def _expert_choice_moe(h, router_w, w_gate, w_up, w_down):
    B, S, D = h.shape
    T = B * S
    cap = max(1, EXPERT_CAPACITY_FACTOR * T // N_EXPERTS)
    xt = h.reshape(T, D)
    affinity = jax.nn.softmax(jnp.einsum('td,de->te', xt, router_w).astype(jnp.float32), axis=-1)
    gate_vals, tok_idx = lax.top_k(affinity.T, cap)
    xs = jnp.take(xt, tok_idx, axis=0)
    hid = jax.nn.silu(jnp.einsum('ecd,edf->ecf', xs, w_gate)) * jnp.einsum('ecd,edf->ecf', xs, w_up)
    ys = jnp.einsum('ecf,efd->ecd', hid, w_down) * gate_vals[..., None].astype(h.dtype)
    out = jnp.zeros((T, D), h.dtype).at[tok_idx.reshape(-1)].add(ys.reshape(-1, D))
    return out.reshape(B, S, D)


def _encoder_layer(x, norm1_g, w_in, q_norm_g, k_norm_g, conv_w, conv_b,
                   lru_w_r, lru_b_r, lru_w_i, lru_b_i, lru_lambda, w_out,
                   norm2_g, router_w, exp_w_gate, exp_w_up, exp_w_down):
    B, S, _ = x.shape
    h = _rmsnorm(x, norm1_g)
    proj = jnp.einsum('bsd,de->bse', h, w_in)
    q, k, v, xr, gate = jnp.split(
        proj, [ATTN_W, ATTN_W + KV_W, ATTN_W + 2 * KV_W, ATTN_W + 2 * KV_W + LRU_W], axis=-1)
    q = _rmsnorm(q.reshape(B, S, N_Q_HEADS, HEAD_DIM), q_norm_g)
    k = _rmsnorm(k.reshape(B, S, N_KV_HEADS, HEAD_DIM), k_norm_g)
    v = v.reshape(B, S, N_KV_HEADS, HEAD_DIM)
    row_ang, col_ang = _axial_angles(S)
    q = _apply_axial_rope(q, row_ang, col_ang)
    k = _apply_axial_rope(k, row_ang, col_ang)
    attn_out = _attention(q, k, v)
    xc = _centred_dwconv(xr, conv_w, conv_b)
    h_lru = (_rglru_direction(xc, lru_w_r[0], lru_b_r[0], lru_w_i[0], lru_b_i[0], lru_lambda[0], False)
             + _rglru_direction(xc, lru_w_r[1], lru_b_r[1], lru_w_i[1], lru_b_i[1], lru_lambda[1], True))
    lru_out = (h_lru * jax.nn.gelu(gate.astype(jnp.float32), approximate=True)).astype(x.dtype)
    mixed = jnp.concatenate([attn_out, lru_out], axis=-1)
    x = x + jnp.einsum('bse,ed->bsd', mixed, w_out)
    h2 = _rmsnorm(x, norm2_g)
    return x + _expert_choice_moe(h2, router_w, exp_w_gate, exp_w_up, exp_w_down)


def _trunk(x, norm1_g, w_in, q_norm_g, k_norm_g, conv_w, conv_b,
           lru_w_r, lru_b_r, lru_w_i, lru_b_i, lru_lambda, w_out,
           norm2_g, router_w, exp_w_gate, exp_w_up, exp_w_down):
    for l in range(DEPTH):
        x = _encoder_layer(x, norm1_g[l], w_in[l], q_norm_g[l], k_norm_g[l], conv_w[l], conv_b[l],
                           lru_w_r[l], lru_b_r[l], lru_w_i[l], lru_b_i[l], lru_lambda[l], w_out[l],
                           norm2_g[l], router_w[l], exp_w_gate[l], exp_w_up[l], exp_w_down[l])
    return x


def setup_inputs(seed: int = 0) -> dict:
    key = jax.random.key(seed)
    ks = jax.random.split(key, 20)
    f32 = jnp.float32
    nrm = lambda k, shape, s: jax.random.normal(k, shape, f32) * s
    u = jax.random.uniform(ks[12], (DEPTH, 2, LRU_W), f32, minval=0.9, maxval=0.999)
    a_base = u ** (1.0 / LRU_C)
    lam = jnp.log(a_base) - jnp.log1p(-a_base)
    return {
        "x_prompt": nrm(ks[0], (BATCH, SEQ, D_MODEL), 1.0),
        "x_sample": nrm(ks[1], (DEC_BATCH, DEC_SEQ, D_MODEL), 1.0),
        "norm1_g": 1.0 + nrm(ks[2], (DEPTH, D_MODEL), 0.02),
        "w_in": nrm(ks[3], (DEPTH, D_MODEL, IN_W), D_MODEL ** -0.5),
        "q_norm_g": 1.0 + nrm(ks[4], (DEPTH, HEAD_DIM), 0.02),
        "k_norm_g": 1.0 + nrm(ks[5], (DEPTH, HEAD_DIM), 0.02),
        "conv_w": nrm(ks[6], (DEPTH, CONV_W, LRU_W), CONV_W ** -0.5),
        "conv_b": nrm(ks[7], (DEPTH, LRU_W), 0.02),
        "lru_w_r": nrm(ks[8], (DEPTH, 2, LRU_BLOCKS, LRU_BLOCK_DIM, LRU_BLOCK_DIM), LRU_BLOCK_DIM ** -0.5),
        "lru_b_r": nrm(ks[9], (DEPTH, 2, LRU_W), 0.02),
        "lru_w_i": nrm(ks[10], (DEPTH, 2, LRU_BLOCKS, LRU_BLOCK_DIM, LRU_BLOCK_DIM), LRU_BLOCK_DIM ** -0.5),
        "lru_b_i": nrm(ks[11], (DEPTH, 2, LRU_W), 0.02),
        "lru_lambda": lam,
        "w_out": nrm(ks[13], (DEPTH, D_MODEL, D_MODEL), D_MODEL ** -0.5),
        "norm2_g": 1.0 + nrm(ks[14], (DEPTH, D_MODEL), 0.02),
        "router_w": nrm(ks[15], (DEPTH, D_MODEL, N_EXPERTS), D_MODEL ** -0.5),
        "exp_w_gate": nrm(ks[16], (DEPTH, N_EXPERTS, D_MODEL, D_EXPERT), D_MODEL ** -0.5),
        "exp_w_up": nrm(ks[17], (DEPTH, N_EXPERTS, D_MODEL, D_EXPERT), D_MODEL ** -0.5),
        "exp_w_down": nrm(ks[18], (DEPTH, N_EXPERTS, D_EXPERT, D_MODEL), D_EXPERT ** -0.5),
    }


def reference(x_prompt, x_sample, norm1_g, w_in, q_norm_g, k_norm_g, conv_w, conv_b,
              lru_w_r, lru_b_r, lru_w_i, lru_b_i, lru_lambda, w_out,
              norm2_g, router_w, exp_w_gate, exp_w_up, exp_w_down):
    y_prompt = _trunk(x_prompt, norm1_g, w_in, q_norm_g, k_norm_g, conv_w, conv_b,
                      lru_w_r, lru_b_r, lru_w_i, lru_b_i, lru_lambda, w_out,
                      norm2_g, router_w, exp_w_gate, exp_w_up, exp_w_down)
    y_sample = _trunk(x_sample, norm1_g, w_in, q_norm_g, k_norm_g, conv_w, conv_b,
                      lru_w_r, lru_b_r, lru_w_i, lru_b_i, lru_lambda, w_out,
                      norm2_g, router_w, exp_w_gate, exp_w_up, exp_w_down)
    return (y_prompt, y_sample)
```

```python
import functools

import jax
import jax.numpy as jnp
from jax import lax
from jax.experimental import pallas as pl
from jax.experimental.pallas import tpu as pltpu

F32 = jnp.float32
BF16 = jnp.bfloat16
I32 = jnp.int32

D_MODEL = 1024
GRID_W = 64
HEAD_DIM = 64
N_Q_HEADS = 8
N_KV_HEADS = 2
ATTN_W = N_Q_HEADS * HEAD_DIM
KV_W = N_KV_HEADS * HEAD_DIM
LRU_W = D_MODEL - ATTN_W
LRU_BLOCKS = 8
LRU_BLOCK_DIM = LRU_W // LRU_BLOCKS
CONV_W = 4
LRU_C = 8.0
N_EXPERTS = 16
EXPERT_CAPACITY_FACTOR = 2
ROPE_THETA = 10000.0
EPS = 1e-6
IN_W = ATTN_W + 2 * KV_W + 2 * LRU_W

LANES = 128
SUBLANES = 8
VMEM_LIMIT = 56 * 1024 * 1024

TOKEN_TILE = 512
Q_TILE = 256
LRU_CHUNK = 256
MOE_BLOCK = 256
EXPERT_TILE = 256
PIECE_SIZES = (256, 128, 64, 32, 16, 8)
P_CHUNK = 32


def _params(sem, vmem=VMEM_LIMIT):
    return pltpu.CompilerParams(dimension_semantics=sem, vmem_limit_bytes=vmem)


def _swap16(x, lane):
    fwd = pltpu.roll(x, 16, axis=1)
    bwd = pltpu.roll(x, LANES - 16, axis=1)
    return jnp.where((lane & 16) == 0, bwd, fwd)


def _split_dot(x, w):
    hi = x.astype(BF16)
    lo = (x - hi.astype(F32)).astype(BF16)
    return (jnp.dot(hi, w, preferred_element_type=F32)
            + jnp.dot(lo, w, preferred_element_type=F32))


def _proj_kernel(x_ref, g1_ref, w_ref, qg_ref, kg_ref, cos_ref, sin_ref, hq_ref, hk_ref,
                 q_ref, ka_ref, kb_ref, va_ref, vb_ref, xr_ref, ga_ref):
    x = x_ref[...]
    ms = jnp.mean(x * x, axis=-1, keepdims=True)
    h = (x * lax.rsqrt(ms + EPS) * g1_ref[...]).astype(BF16)
    proj = jnp.dot(h, w_ref[...], preferred_element_type=F32)
    q = proj[:, :ATTN_W]
    k = proj[:, ATTN_W:ATTN_W + KV_W]
    v = proj[:, ATTN_W + KV_W:ATTN_W + 2 * KV_W]
    xr = proj[:, ATTN_W + 2 * KV_W:ATTN_W + 2 * KV_W + LRU_W]
    gate = proj[:, ATTN_W + 2 * KV_W + LRU_W:]

    cos = cos_ref[...]
    sin = sin_ref[...]
    lane = lax.broadcasted_iota(I32, cos.shape, 1)

    qn = q * lax.rsqrt(_split_dot(q * q, hq_ref[...]) + EPS) * qg_ref[...]
    kn = k * lax.rsqrt(_split_dot(k * k, hk_ref[...]) + EPS) * kg_ref[...]

    qscale = (HEAD_DIM ** -0.5) * 1.4426950408889634
    for p in range(ATTN_W // LANES):
        qc = qn[:, p * LANES:(p + 1) * LANES]
        qr = qc * cos + _swap16(qc, lane) * sin
        q_ref[:, p * LANES:(p + 1) * LANES] = (qr * qscale).astype(BF16)
    kr = kn * cos + _swap16(kn, lane) * sin
    ka_ref[...] = kr.astype(BF16)
    kb_ref[...] = pltpu.roll(kr, HEAD_DIM, axis=1).astype(BF16)
    va_ref[...] = v.astype(BF16)
    vb_ref[...] = pltpu.roll(v, HEAD_DIM, axis=1).astype(BF16)
    xr_ref[...] = xr
    ga_ref[...] = 0.5 * gate * (1.0 + jnp.tanh(0.7978845608028654 * (gate + 0.044715 * gate * gate * gate)))


def _proj(x2d, seq, g1, w_in, qg, kg, cos, sin, hq, hk):
    T = x2d.shape[0]
    tm = min(TOKEN_TILE, seq)
    nseq = seq // tm
    row = lambda i: (i, 0)
    const = lambda i: (0, 0)
    pos = lambda i: (i % nseq, 0)
    out_shapes = (
        jax.ShapeDtypeStruct((T, ATTN_W), BF16),
        jax.ShapeDtypeStruct((T, KV_W), BF16),
        jax.ShapeDtypeStruct((T, KV_W), BF16),
        jax.ShapeDtypeStruct((T, KV_W), BF16),
        jax.ShapeDtypeStruct((T, KV_W), BF16),
        jax.ShapeDtypeStruct((T, LRU_W), F32),
        jax.ShapeDtypeStruct((T, LRU_W), F32),
    )
    return pl.pallas_call(
        _proj_kernel,
        grid=(T // tm,),
        in_specs=[
            pl.BlockSpec((tm, D_MODEL), row),
            pl.BlockSpec((1, D_MODEL), const),
            pl.BlockSpec((D_MODEL, IN_W), const),
            pl.BlockSpec((1, ATTN_W), const),
            pl.BlockSpec((1, KV_W), const),
            pl.BlockSpec((tm, LANES), pos),
            pl.BlockSpec((tm, LANES), pos),
            pl.BlockSpec((ATTN_W, ATTN_W), const),
            pl.BlockSpec((KV_W, KV_W), const),
        ],
        out_specs=(
            pl.BlockSpec((tm, ATTN_W), row),
            pl.BlockSpec((tm, KV_W), row),
            pl.BlockSpec((tm, KV_W), row),
            pl.BlockSpec((tm, KV_W), row),
            pl.BlockSpec((tm, KV_W), row),
            pl.BlockSpec((tm, LRU_W), row),
            pl.BlockSpec((tm, LRU_W), row),
        ),
        out_shape=out_shapes,
        compiler_params=_params(("parallel",)),
        name="proj",
    )(x2d, g1, w_in, qg, kg, cos, sin, hq, hk)


def _sigmoid(x):
    return 1.0 / (1.0 + jnp.exp(-x))


def _scan8(a, b, h_in, first_row, shifts):
    b = b + jnp.where(first_row, a * h_in, 0.0)
    a = jnp.where(first_row, 0.0, a)
    for d in shifts:
        b = b + a * pltpu.roll(b, d, axis=0)
        a = a * pltpu.roll(a, d, axis=0)
    return b


def _lru_kernel(xr_ref, ga_ref, cw_ref, cb_ref, wbd_ref, bias_ref, lam_ref, o_ref,
                xpad_s, xc_s, hf_s, pre_f, pre_b):
    S = xr_ref.shape[0]
    C = LRU_W
    CH = min(LRU_CHUNK, S)
    NC = S // CH
    G = CH // SUBLANES
    left = CONV_W // 2

    xpad_s[0:SUBLANES, :] = jnp.zeros((SUBLANES, C), F32)
    xpad_s[SUBLANES + S:2 * SUBLANES + S, :] = jnp.zeros((SUBLANES, C), F32)
    xpad_s[SUBLANES:SUBLANES + S, :] = xr_ref[...]
    for c in range(NC):
        u = xpad_s[c * CH:c * CH + CH + 2 * SUBLANES, :]
        acc = cb_ref[...] + jnp.zeros((CH, C), F32)
        for tap in range(CONV_W):
            shift = (left - tap) % (CH + 2 * SUBLANES)
            ush = u if shift == 0 else pltpu.roll(u, shift, axis=0)
            acc = acc + ush[SUBLANES:SUBLANES + CH, :] * cw_ref[tap:tap + 1, :]
        xc_s[c * CH:(c + 1) * CH, :] = acc

    z = -lam_ref[...]
    nsp = -LRU_C * (jnp.maximum(z, 0.0) + jnp.log1p(jnp.exp(-jnp.abs(z))))
    nsp_f = nsp[0:1, :]
    nsp_b = nsp[1:2, :]
    row = lax.broadcasted_iota(I32, (SUBLANES, C), 0)
    is_first = row == 0
    is_last = row == SUBLANES - 1

    def gates(pre_ref, g, x8, nsp_d):
        r = _sigmoid(pre_ref[pl.ds(g * SUBLANES, SUBLANES), 0:C])
        i = _sigmoid(pre_ref[pl.ds(g * SUBLANES, SUBLANES), C:2 * C])
        log_a = nsp_d * r
        a = jnp.exp(log_a)
        mult = jnp.sqrt(-jnp.tanh(log_a) * (a * a + 1.0))
        return a, mult * i * x8

    hf = jnp.zeros((SUBLANES, C), F32)
    hb = jnp.zeros((SUBLANES, C), F32)
    for c in range(NC):
        cf = c
        cb = NC - 1 - c
        xf = xc_s[cf * CH:(cf + 1) * CH, :].astype(BF16)
        pre_f[...] = jnp.dot(xf, wbd_ref[:, 0:2 * C], preferred_element_type=F32) + bias_ref[:, 0:2 * C]
        xb = xc_s[cb * CH:(cb + 1) * CH, :].astype(BF16)
        pre_b[...] = jnp.dot(xb, wbd_ref[:, 2 * C:4 * C], preferred_element_type=F32) + bias_ref[:, 2 * C:4 * C]

        def body(g, carry, cf=cf, cb=cb):
            hf, hb = carry
            rf = pl.multiple_of(cf * CH + g * SUBLANES, SUBLANES)
            a, b = gates(pre_f, g, xc_s[pl.ds(rf, SUBLANES), :], nsp_f)
            h = _scan8(a, b, hf, is_first, (1, 2, 4))
            hf_s[pl.ds(rf, SUBLANES), :] = h
            hf = jnp.broadcast_to(h[SUBLANES - 1:SUBLANES, :], (SUBLANES, C))
            gb = G - 1 - g
            rb = pl.multiple_of(cb * CH + gb * SUBLANES, SUBLANES)
            a, b = gates(pre_b, gb, xc_s[pl.ds(rb, SUBLANES), :], nsp_b)
            h = _scan8(a, b, hb, is_last, (SUBLANES - 1, SUBLANES - 2, SUBLANES - 4))
            o_ref[pl.ds(rb, SUBLANES), :] = h
            hb = jnp.broadcast_to(h[0:1, :], (SUBLANES, C))
            return hf, hb

        hf, hb = lax.fori_loop(0, G, body, (hf, hb))

    o_ref[...] = (hf_s[...] + o_ref[...]) * ga_ref[...]


def _lru(xr, ga, cw, cb, wbd, bias, lam):
    B, S, C = xr.shape
    CH = min(LRU_CHUNK, S)
    blk = lambda b: (b, 0, 0)
    const = lambda b: (0, 0)
    return pl.pallas_call(
        _lru_kernel,
        grid=(B,),
        in_specs=[
            pl.BlockSpec((None, S, C), blk),
            pl.BlockSpec((None, S, C), blk),
            pl.BlockSpec((CONV_W, C), const),
            pl.BlockSpec((1, C), const),
            pl.BlockSpec((C, 4 * C), const),
            pl.BlockSpec((1, 4 * C), const),
            pl.BlockSpec((2, C), const),
        ],
        out_specs=pl.BlockSpec((None, S, C), blk),
        out_shape=jax.ShapeDtypeStruct((B, S, C), F32),
        scratch_shapes=[
            pltpu.VMEM((S + 2 * SUBLANES, C), F32),
            pltpu.VMEM((S, C), F32),
            pltpu.VMEM((S, C), F32),
            pltpu.VMEM((CH, 2 * C), F32),
            pltpu.VMEM((CH, 2 * C), F32),
        ],
        compiler_params=_params(("parallel",)),
        name="lru",
    )(xr, ga, cw, cb, wbd, bias, lam)


def _attn_kernel(q_ref, ka_ref, kb_ref, va_ref, vb_ref, o_ref):
    tq = q_ref.shape[0]
    lane = lax.broadcasted_iota(I32, (tq, LANES), 1)
    low = lane < HEAD_DIM
    contract_last = (((1,), (1,)), ((), ()))
    for p in range(ATTN_W // LANES):
        kv = (2 * p) // (N_Q_HEADS // N_KV_HEADS)
        qp = q_ref[:, p * LANES:(p + 1) * LANES]
        halves = []
        for half in range(2):
            keep = low if half == 0 else jnp.logical_not(low)
            qm = jnp.where(keep, qp, jnp.zeros_like(qp))
            k_ref = ka_ref if kv == half else kb_ref
            v_ref = va_ref if kv == half else vb_ref
            s = lax.dot_general(qm, k_ref[...], contract_last, preferred_element_type=F32)
            m = jnp.max(s, axis=-1, keepdims=True)
            pr = jnp.exp2(s - m)
            l = jnp.sum(pr, axis=-1, keepdims=True)
            o = jnp.dot(pr.astype(BF16), v_ref[...], preferred_element_type=F32)
            halves.append(o * (1.0 / l))
        o_ref[:, p * LANES:(p + 1) * LANES] = jnp.where(low, halves[0], halves[1]).astype(BF16)


def _attn(q, ka, kb, va, vb):
    B, S, _ = q.shape
    tq = min(Q_TILE, S)
    qblk = lambda b, i: (b, i, 0)
    kblk = lambda b, i: (b, 0, 0)
    return pl.pallas_call(
        _attn_kernel,
        grid=(B, S // tq),
        in_specs=[
            pl.BlockSpec((None, tq, ATTN_W), qblk),
            pl.BlockSpec((None, S, KV_W), kblk),
            pl.BlockSpec((None, S, KV_W), kblk),
            pl.BlockSpec((None, S, KV_W), kblk),
            pl.BlockSpec((None, S, KV_W), kblk),
        ],
        out_specs=pl.BlockSpec((None, tq, ATTN_W), qblk),
        out_shape=jax.ShapeDtypeStruct((B, S, ATTN_W), BF16),
        compiler_params=_params(("parallel", "parallel")),
        name="attn",
    )(q, ka, kb, va, vb)


def _outproj_kernel(a_ref, l_ref, x_ref, wa_ref, wl_ref, g2_ref, rw_ref, x1_ref, h2_ref, aff_ref):
    x1 = (x_ref[...]
          + jnp.dot(a_ref[...], wa_ref[...], preferred_element_type=F32)
          + jnp.dot(l_ref[...].astype(BF16), wl_ref[...], preferred_element_type=F32))
    x1_ref[...] = x1
    ms = jnp.mean(x1 * x1, axis=-1, keepdims=True)
    h2 = (x1 * lax.rsqrt(ms + EPS) * g2_ref[...]).astype(BF16)
    h2_ref[...] = h2
    logits = lax.dot_general(rw_ref[...], h2, (((1,), (1,)), ((), ())), preferred_element_type=F32)
    m = jnp.max(logits, axis=0, keepdims=True)
    e = jnp.exp(logits - m)
    aff_ref[...] = e / jnp.sum(e, axis=0, keepdims=True)


def _outproj(attn2d, lru2d, x2d, wa, wl, g2, rwt):
    T = x2d.shape[0]
    tm = min(TOKEN_TILE, T)
    row = lambda i: (i, 0)
    const = lambda i: (0, 0)
    return pl.pallas_call(
        _outproj_kernel,
        grid=(T // tm,),
        in_specs=[
            pl.BlockSpec((tm, ATTN_W), row),
            pl.BlockSpec((tm, LRU_W), row),
            pl.BlockSpec((tm, D_MODEL), row),
            pl.BlockSpec((ATTN_W, D_MODEL), const),
            pl.BlockSpec((LRU_W, D_MODEL), const),
            pl.BlockSpec((1, D_MODEL), const),
            pl.BlockSpec((N_EXPERTS, D_MODEL), const),
        ],
        out_specs=(
            pl.BlockSpec((tm, D_MODEL), row),
            pl.BlockSpec((tm, D_MODEL), row),
            pl.BlockSpec((N_EXPERTS, tm), lambda i: (0, i)),
        ),
        out_shape=(
            jax.ShapeDtypeStruct((T, D_MODEL), F32),
            jax.ShapeDtypeStruct((T, D_MODEL), BF16),
            jax.ShapeDtypeStruct((N_EXPERTS, T), F32),
        ),
        compiler_params=_params(("parallel",)),
        name="outproj",
    )(attn2d, lru2d, x2d, wa, wl, g2, rwt)


def _select_kernel(aff_ref, tri_ref, trinb_ref, trie_ref, gsel_ref, cnt_ref, off_ref, base_ref, tot_ref,
                   used_ref, *, cap, T, NB):
    TB = T // NB
    CW = min(2048, T)
    E = N_EXPERTS

    def count_where(pred):
        def body(i, acc):
            bits = pltpu.bitcast(aff_ref[:, pl.ds(pl.multiple_of(i * CW, LANES), CW)], I32)
            m = pred(bits).astype(I32)
            part = m[:, 0:LANES]
            for qd in range(1, CW // LANES):
                part = part + m[:, qd * LANES:(qd + 1) * LANES]
            return acc + part
        acc = lax.fori_loop(0, T // CW, body, jnp.zeros((E, LANES), I32))
        return jnp.sum(acc, axis=1, keepdims=True)

    def bit_step(it, thr):
        cand = thr | jnp.left_shift(jnp.int32(1), 30 - it)
        n = count_where(lambda bits: bits >= cand)
        return jnp.where(n >= cap, cand, thr)

    thr = lax.fori_loop(0, 31, bit_step, jnp.zeros((E, 1), I32))
    n_gt = count_where(lambda bits: bits > thr)
    need = (cap - n_gt).astype(F32)

    lane_nb = lax.broadcasted_iota(I32, (E, NB), 1)

    def blk_step(j, carry):
        eq_before, cnt_acc = carry
        aff = aff_ref[:, pl.ds(pl.multiple_of(j * TB, LANES), TB)]
        bits = pltpu.bitcast(aff, I32)
        gt = bits > thr
        eq = bits == thr
        eqf = jnp.where(eq, 1.0, 0.0)
        pref = jnp.dot(eqf.astype(BF16), tri_ref[...], preferred_element_type=F32)
        take = jnp.logical_and(eq, (eq_before + pref) < need)
        sel = jnp.logical_or(gt, take)
        gsel_ref[:, pl.ds(pl.multiple_of(j * TB, LANES), TB)] = jnp.where(sel, aff, -1.0)
        cnt = jnp.sum(jnp.where(sel, 1.0, 0.0), axis=1, keepdims=True)
        cnt_acc = jnp.where(lane_nb == j, cnt, cnt_acc)
        return eq_before + jnp.sum(eqf, axis=1, keepdims=True), cnt_acc

    _, cnt = lax.fori_loop(0, NB, blk_step, (jnp.zeros((E, 1), F32), jnp.zeros((E, NB), F32)))
    npad = jnp.floor((cnt + (SUBLANES - 1)) * (1.0 / SUBLANES)) * SUBLANES
    npad_b = npad.astype(BF16)
    off = jnp.dot(npad_b, trinb_ref[...], preferred_element_type=F32)
    base = jnp.dot(trie_ref[...], npad_b, preferred_element_type=F32)
    cnt_ref[...] = cnt.astype(I32)
    off_ref[...] = off.astype(I32)
    base_ref[...] = base.astype(I32)
    tot_ref[...] = jnp.broadcast_to(jnp.sum(npad, axis=0, keepdims=True), (SUBLANES, NB)).astype(I32)
    used_ref[...] = jnp.broadcast_to(jnp.sum(npad, axis=1, keepdims=True), (E, LANES)).astype(I32)


def _strict_upper(n):
    r = lax.broadcasted_iota(I32, (n, n), 0)
    c = lax.broadcasted_iota(I32, (n, n), 1)
    return (r < c).astype(BF16)


def _select(aff_t, cap, NB):
    E, T = aff_t.shape
    TB = T // NB
    tri = _strict_upper(TB)
    trinb = _strict_upper(NB)
    trie = _strict_upper(E).T
    full = lambda shape: pl.BlockSpec(shape, lambda i: (0,) * len(shape))
    return pl.pallas_call(
        functools.partial(_select_kernel, cap=cap, T=T, NB=NB),
        grid=(1,),
        in_specs=[full((E, T)), full((TB, TB)), full((NB, NB)), full((E, E))],
        out_specs=(full((E, T)), full((E, NB)), full((E, NB)), full((E, NB)), full((SUBLANES, NB)),
                   full((E, LANES))),
        out_shape=(
            jax.ShapeDtypeStruct((E, T), F32),
            jax.ShapeDtypeStruct((E, NB), I32),
            jax.ShapeDtypeStruct((E, NB), I32),
            jax.ShapeDtypeStruct((E, NB), I32),
            jax.ShapeDtypeStruct((SUBLANES, NB), I32),
            jax.ShapeDtypeStruct((E, LANES), I32),
        ),
        compiler_params=_params(("arbitrary",)),
        name="select",
    )(aff_t, tri, trinb, trie)


def _ceil8(n):
    return ((n + (SUBLANES - 1)) // SUBLANES) * SUBLANES


def _build_onehot(j, NB, cnt_s, base_s, pos_s, val_s, p_buf):
    TB = p_buf.shape[1]
    for e in range(N_EXPERTS):
        npad = _ceil8(cnt_s[e * NB + j])
        base = base_s[e * NB + j]
        prow = pos_s[e:e + 1, :]
        vrow = val_s[e:e + 1, :]

        def chunk(qd, carry, prow=prow, vrow=vrow, base=base):
            slot = (qd * P_CHUNK + lax.broadcasted_iota(I32, (P_CHUNK, TB), 0)).astype(F32)
            start = pl.multiple_of(base + qd * P_CHUNK, SUBLANES)
            p_buf[pl.ds(start, P_CHUNK), :] = jnp.where(prow == slot, vrow, 0.0)
            return carry

        lax.fori_loop(0, (npad + (P_CHUNK - 1)) // P_CHUNK, chunk, 0)


def _piece_copies(j, NB, cnt_s, off_s, base_s, buf, lst, sem, to_list, start):
    for e in range(N_EXPERTS):
        npad = _ceil8(cnt_s[e * NB + j])
        base = base_s[e * NB + j]
        off = off_s[e * NB + j]
        for sz in PIECE_SIZES:
            done = (npad // (2 * sz)) * (2 * sz)

            @pl.when((npad & sz) != 0)
            def _(e=e, sz=sz, done=done, base=base, off=off):
                b = buf.at[pl.ds(pl.multiple_of(base + done, SUBLANES), sz)]
                l = lst.at[e, pl.ds(pl.multiple_of(off + done, SUBLANES), sz)]
                cp = pltpu.make_async_copy(b, l, sem) if to_list else pltpu.make_async_copy(l, b, sem)
                if start:
                    cp.start()
                else:
                    cp.wait()


def _gather_kernel(cnt_s, off_s, base_s, tot_s, used_s, h2_ref, gsel_ref, tri_ref, list_ref,
                   pos_s, val_s, p_buf, out_buf, zero_buf, sem, *, NB):
    j = pl.program_id(0)

    @pl.when(j == 0)
    def _():
        p_buf[...] = jnp.zeros(p_buf.shape, F32)

    sel = gsel_ref[...] >= 0.0
    self_ = jnp.where(sel, 1.0, 0.0)
    pos_s[...] = jnp.dot(self_.astype(BF16), tri_ref[...], preferred_element_type=F32)
    val_s[...] = self_
    _build_onehot(j, NB, cnt_s, base_s, pos_s, val_s, p_buf)

    def mm(c, carry):
        r0 = pl.multiple_of(c * MOE_BLOCK, MOE_BLOCK)
        pc = p_buf[pl.ds(r0, MOE_BLOCK), :].astype(BF16)
        out_buf[pl.ds(r0, MOE_BLOCK), :] = jnp.dot(pc, h2_ref[...], preferred_element_type=F32)
        return carry

    lax.fori_loop(0, (tot_s[j] + (MOE_BLOCK - 1)) // MOE_BLOCK, mm, 0)
    _piece_copies(j, NB, cnt_s, off_s, base_s, out_buf, list_ref, sem, True, True)
    _piece_copies(j, NB, cnt_s, off_s, base_s, out_buf, list_ref, sem, True, False)

    @pl.when(j == NB - 1)
    def _():
        zero_buf[...] = jnp.zeros(zero_buf.shape, F32)
        for e in range(N_EXPERTS):
            dst = list_ref.at[e, pl.ds(pl.multiple_of(used_s[e], SUBLANES), EXPERT_TILE)]
            pltpu.make_async_copy(zero_buf, dst, sem).start()
        for e in range(N_EXPERTS):
            dst = list_ref.at[e, pl.ds(pl.multiple_of(used_s[e], SUBLANES), EXPERT_TILE)]
            pltpu.make_async_copy(zero_buf, dst, sem).wait()


def _gather(tables, h2, gsel, L):
    cnt, off, base, tot, used = tables
    T = h2.shape[0]
    TB = min(MOE_BLOCK, T)
    NB = T // TB
    rmax = N_EXPERTS * TB
    tri = _strict_upper(TB)
    grid_spec = pltpu.PrefetchScalarGridSpec(
        num_scalar_prefetch=5,
        grid=(NB,),
        in_specs=[
            pl.BlockSpec((TB, D_MODEL), lambda j, *_: (j, 0)),
            pl.BlockSpec((N_EXPERTS, TB), lambda j, *_: (0, j)),
            pl.BlockSpec((TB, TB), lambda j, *_: (0, 0)),
        ],
        out_specs=pl.BlockSpec(memory_space=pl.ANY),
        scratch_shapes=[
            pltpu.VMEM((N_EXPERTS, TB), F32),
            pltpu.VMEM((N_EXPERTS, TB), F32),
            pltpu.VMEM((rmax + P_CHUNK, TB), F32),
            pltpu.VMEM((rmax, D_MODEL), F32),
            pltpu.VMEM((EXPERT_TILE, D_MODEL), F32),
            pltpu.SemaphoreType.DMA,
        ],
    )
    return pl.pallas_call(
        functools.partial(_gather_kernel, NB=NB),
        grid_spec=grid_spec,
        out_shape=jax.ShapeDtypeStruct((N_EXPERTS, L, D_MODEL), F32),
        compiler_params=_params(("arbitrary",)),
        name="gather",
    )(cnt, off, base, tot, used, h2, gsel, tri)


def _experts_kernel(used_s, xs_ref, wg_ref, wu_ref, wd_ref, ys_ref):
    e = pl.program_id(0)
    c = pl.program_id(1)

    @pl.when(c * EXPERT_TILE < used_s[e])
    def _():
        xs = xs_ref[...].astype(BF16)
        hg = jnp.dot(xs, wg_ref[...], preferred_element_type=F32)
        hu = jnp.dot(xs, wu_ref[...], preferred_element_type=F32)
        hid = (hg * _sigmoid(hg) * hu).astype(BF16)
        ys_ref[...] = jnp.dot(hid, wd_ref[...], preferred_element_type=F32)


def _experts(used, xs, wg, wu, wd):
    E, L, D = xs.shape
    NT = L // EXPERT_TILE

    def tile(e, c, used):
        last = (used[e] + (EXPERT_TILE - 1)) // EXPERT_TILE - 1
        return (e, jnp.minimum(c, last), 0)

    wblk = lambda e, c, used: (e, 0, 0)
    grid_spec = pltpu.PrefetchScalarGridSpec(
        num_scalar_prefetch=1,
        grid=(E, NT),
        in_specs=[
            pl.BlockSpec((None, EXPERT_TILE, D), tile),
            pl.BlockSpec((None, D, D), wblk),
            pl.BlockSpec((None, D, D), wblk),
            pl.BlockSpec((None, D, D), wblk),
        ],
        out_specs=pl.BlockSpec((None, EXPERT_TILE, D), tile),
    )
    return pl.pallas_call(
        _experts_kernel,
        grid_spec=grid_spec,
        out_shape=jax.ShapeDtypeStruct((E, L, D), F32),
        compiler_params=_params(("arbitrary", "arbitrary")),
        name="experts",
    )(used, xs, wg, wu, wd)


def _combine_kernel(cnt_s, off_s, base_s, tot_s, x1_ref, gsel_ref, tri_ref, ys_ref, y_ref,
                    pos_s, val_s, p_buf, in_buf, sem, *, NB):
    j = pl.program_id(0)

    @pl.when(j == 0)
    def _():
        p_buf[...] = jnp.zeros(p_buf.shape, F32)
        in_buf[...] = jnp.zeros(in_buf.shape, F32)

    _piece_copies(j, NB, cnt_s, off_s, base_s, in_buf, ys_ref, sem, False, True)
    g = gsel_ref[...]
    sel = g >= 0.0
    pos_s[...] = jnp.dot(jnp.where(sel, 1.0, 0.0).astype(BF16), tri_ref[...], preferred_element_type=F32)
    val_s[...] = jnp.where(sel, g, 0.0)
    _build_onehot(j, NB, cnt_s, base_s, pos_s, val_s, p_buf)
    tail = pl.multiple_of(tot_s[j], SUBLANES)
    p_buf[pl.ds(tail, MOE_BLOCK + P_CHUNK), :] = jnp.zeros((MOE_BLOCK + P_CHUNK, p_buf.shape[1]), F32)
    _piece_copies(j, NB, cnt_s, off_s, base_s, in_buf, ys_ref, sem, False, False)

    def mm(c, acc):
        r0 = pl.multiple_of(c * MOE_BLOCK, MOE_BLOCK)
        pc = p_buf[pl.ds(r0, MOE_BLOCK), :].astype(BF16)
        yc = in_buf[pl.ds(r0, MOE_BLOCK), :].astype(BF16)
        return acc + lax.dot_general(pc, yc, (((0,), (0,)), ((), ())), preferred_element_type=F32)

    y_ref[...] = lax.fori_loop(0, (tot_s[j] + (MOE_BLOCK - 1)) // MOE_BLOCK, mm, x1_ref[...])


def _combine(tables, x1, gsel, ys):
    cnt, off, base, tot, _ = tables
    T = x1.shape[0]
    TB = min(MOE_BLOCK, T)
    NB = T // TB
    rmax = N_EXPERTS * TB
    tri = _strict_upper(TB)
    grid_spec = pltpu.PrefetchScalarGridSpec(
        num_scalar_prefetch=4,
        grid=(NB,),
        in_specs=[
            pl.BlockSpec((TB, D_MODEL), lambda j, *_: (j, 0)),
            pl.BlockSpec((N_EXPERTS, TB), lambda j, *_: (0, j)),
            pl.BlockSpec((TB, TB), lambda j, *_: (0, 0)),
            pl.BlockSpec(memory_space=pl.ANY),
        ],
        out_specs=pl.BlockSpec((TB, D_MODEL), lambda j, *_: (j, 0)),
        scratch_shapes=[
            pltpu.VMEM((N_EXPERTS, TB), F32),
            pltpu.VMEM((N_EXPERTS, TB), F32),
            pltpu.VMEM((rmax + MOE_BLOCK + P_CHUNK, TB), F32),
            pltpu.VMEM((rmax, D_MODEL), F32),
            pltpu.SemaphoreType.DMA,
        ],
    )
    return pl.pallas_call(
        functools.partial(_combine_kernel, NB=NB),
        grid_spec=grid_spec,
        out_shape=jax.ShapeDtypeStruct((T, D_MODEL), F32),
        compiler_params=_params(("arbitrary",)),
        name="combine",
    )(cnt, off, base, tot, x1, gsel, tri, ys)


def _rope_tables(seq):
    pos = jnp.arange(seq, dtype=I32)
    row = (pos // GRID_W).astype(F32)
    col = (pos % GRID_W).astype(F32)
    axis_dim = HEAD_DIM // 2
    inv_freq = ROPE_THETA ** (-jnp.arange(0, axis_dim, 2, dtype=F32) / axis_dim)
    ra = row[:, None] * inv_freq[None, :]
    ca = col[:, None] * inv_freq[None, :]
    cos = jnp.concatenate([jnp.cos(ra), jnp.cos(ra), jnp.cos(ca), jnp.cos(ca)], axis=-1)
    sin = jnp.concatenate([-jnp.sin(ra), jnp.sin(ra), -jnp.sin(ca), jnp.sin(ca)], axis=-1)
    reps = LANES // HEAD_DIM
    return jnp.tile(cos, (1, reps)), jnp.tile(sin, (1, reps))


def _head_mean_matrix(width):
    r = lax.broadcasted_iota(I32, (width, width), 0) // HEAD_DIM
    c = lax.broadcasted_iota(I32, (width, width), 1) // HEAD_DIM
    return jnp.where(r == c, 1.0 / HEAD_DIM, 0.0).astype(BF16)


def _block_diag(w):
    n, d, _ = w.shape
    eye = jnp.eye(n, dtype=w.dtype)
    return jnp.einsum('nde,nm->ndme', w, eye).reshape(n * d, n * d)


def _layer(x, weights):
    (g1, w_in, qg, kg, cw, cb, wbd, bias, lam, wa, wl, g2, rwt, wg, wu, wd) = weights
    B, S, D = x.shape
    T = B * S
    x2d = x.reshape(T, D)
    cos, sin = _rope_tables(S)
    hq = _head_mean_matrix(ATTN_W)
    hk = _head_mean_matrix(KV_W)
    q, ka, kb, va, vb, xr, ga = _proj(x2d, S, g1, w_in, qg, kg, cos, sin, hq, hk)
    lru = _lru(xr.reshape(B, S, LRU_W), ga.reshape(B, S, LRU_W), cw, cb, wbd, bias, lam)
    att = _attn(q.reshape(B, S, ATTN_W), ka.reshape(B, S, KV_W), kb.reshape(B, S, KV_W),
                va.reshape(B, S, KV_W), vb.reshape(B, S, KV_W))
    x1, h2, aff_t = _outproj(att.reshape(T, ATTN_W), lru.reshape(T, LRU_W), x2d, wa, wl, g2, rwt)

    cap = max(1, EXPERT_CAPACITY_FACTOR * T // N_EXPERTS)
    TB = min(MOE_BLOCK, T)
    NB = T // TB
    L = -(-(cap + SUBLANES * NB + EXPERT_TILE) // EXPERT_TILE) * EXPERT_TILE
    gsel, cnt, off, base, tot, used = _select(aff_t, cap, NB)
    tables = (cnt.reshape(-1), off.reshape(-1), base.reshape(-1), tot[0], used[:, 0])
    xs = _gather(tables, h2, gsel, L)
    ys = _experts(tables[4], xs, wg, wu, wd)
    y = _combine(tables, x1, gsel, ys)
    return y.reshape(B, S, D)


def kernel(x_prompt, x_sample, norm1_g, w_in, q_norm_g, k_norm_g, conv_w, conv_b, lru_w_r, lru_b_r,
           lru_w_i, lru_b_i, lru_lambda, w_out, norm2_g, router_w, exp_w_gate, exp_w_up, exp_w_down):
    layers = []
    for l in range(norm1_g.shape[0]):
        wbd = jnp.concatenate(
            [_block_diag(lru_w_r[l, 0]), _block_diag(lru_w_i[l, 0]),
             _block_diag(lru_w_r[l, 1]), _block_diag(lru_w_i[l, 1])], axis=1).astype(BF16)
        bias = jnp.concatenate([lru_b_r[l, 0], lru_b_i[l, 0], lru_b_r[l, 1], lru_b_i[l, 1]])[None, :]
        layers.append((
            norm1_g[l][None, :], w_in[l].astype(BF16),
            jnp.tile(q_norm_g[l], N_Q_HEADS)[None, :], jnp.tile(k_norm_g[l], N_KV_HEADS)[None, :],
            conv_w[l], conv_b[l][None, :], wbd, bias, lru_lambda[l],
            w_out[l, :ATTN_W].astype(BF16), w_out[l, ATTN_W:].astype(BF16),
            norm2_g[l][None, :], router_w[l].T.astype(BF16),
            exp_w_gate[l].astype(BF16), exp_w_up[l].astype(BF16), exp_w_down[l].astype(BF16),
        ))
    ys = []
    for x in (x_prompt, x_sample):
        for weights in layers:
            x = _layer(x, weights)
        ys.append(x)
    return tuple(ys)
```

```python
import functools

import jax
import jax.numpy as jnp
from jax import lax
from jax.experimental import pallas as pl
from jax.experimental.pallas import tpu as pltpu

F32 = jnp.float32
BF16 = jnp.bfloat16
I32 = jnp.int32
U32 = jnp.uint32

D_MODEL = 1024
GRID_W = 64
HEAD_DIM = 64
N_Q_HEADS = 8
N_KV_HEADS = 2
ATTN_W = N_Q_HEADS * HEAD_DIM
KV_W = N_KV_HEADS * HEAD_DIM
LRU_W = D_MODEL - ATTN_W
LRU_BLOCKS = 8
LRU_BLOCK_DIM = LRU_W // LRU_BLOCKS
CONV_W = 4
LRU_C = 8.0
N_EXPERTS = 16
EXPERT_CAPACITY_FACTOR = 2
ROPE_THETA = 10000.0
EPS = 1e-6
IN_W = ATTN_W + 2 * KV_W + 2 * LRU_W
LOG2E = 1.4426950408889634

LANES = 128
SUBLANES = 8
VMEM_LIMIT = 56 * 1024 * 1024

TOKEN_TILE = 1024
Q_TILE = 512
LRU_CHUNK = 512
LRU_GROUPS = LRU_W // LANES
MOE_BLOCK = 256
GATHER_ROWS = 1024
EXPERT_TILE = 512
ZERO_ROWS = 256
SLOT_SMALL = 64
P_CHUNK = 32
PACKED_W = D_MODEL // 2


def _params(sem, vmem=VMEM_LIMIT):
    return pltpu.CompilerParams(dimension_semantics=sem, vmem_limit_bytes=vmem)


def _swap16(x, lane):
    fwd = pltpu.roll(x, 16, axis=1)
    bwd = pltpu.roll(x, LANES - 16, axis=1)
    return jnp.where((lane & 16) == 0, bwd, fwd)


def _split_dot(x, w):
    hi = x.astype(BF16)
    lo = (x - hi.astype(F32)).astype(BF16)
    return (jnp.dot(hi, w, preferred_element_type=F32)
            + jnp.dot(lo, w, preferred_element_type=F32))


def _proj_kernel(x_ref, g1_ref, w_ref, qg_ref, kg_ref, cos_ref, sin_ref, hq_ref, hk_ref,
                 q_ref, ka_ref, kb_ref, va_ref, vb_ref, xr_ref, ga_ref):
    x = x_ref[...]
    ms = jnp.mean(x * x, axis=-1, keepdims=True)
    h = (x * lax.rsqrt(ms + EPS) * g1_ref[...]).astype(BF16)
    proj = jnp.dot(h, w_ref[...], preferred_element_type=F32)
    q = proj[:, :ATTN_W]
    k = proj[:, ATTN_W:ATTN_W + KV_W]
    v = proj[:, ATTN_W + KV_W:ATTN_W + 2 * KV_W]
    xr = proj[:, ATTN_W + 2 * KV_W:ATTN_W + 2 * KV_W + LRU_W]
    gate = proj[:, ATTN_W + 2 * KV_W + LRU_W:]

    cos = cos_ref[...]
    sin = sin_ref[...]
    lane = lax.broadcasted_iota(I32, cos.shape, 1)

    qn = q * lax.rsqrt(_split_dot(q * q, hq_ref[...]) + EPS) * qg_ref[...]
    kn = k * lax.rsqrt(_split_dot(k * k, hk_ref[...]) + EPS) * kg_ref[...]

    qscale = (HEAD_DIM ** -0.5) * LOG2E
    for p in range(ATTN_W // LANES):
        qc = qn[:, p * LANES:(p + 1) * LANES]
        qr = qc * cos + _swap16(qc, lane) * sin
        q_ref[:, p * LANES:(p + 1) * LANES] = (qr * qscale).astype(BF16)
    kr = kn * cos + _swap16(kn, lane) * sin
    ka_ref[...] = kr.astype(BF16)
    kb_ref[...] = pltpu.roll(kr, HEAD_DIM, axis=1).astype(BF16)
    va_ref[...] = v.astype(BF16)
    vb_ref[...] = pltpu.roll(v, HEAD_DIM, axis=1).astype(BF16)
    xr_ref[...] = xr
    ga_ref[...] = 0.5 * gate * (1.0 + jnp.tanh(0.7978845608028654 * (gate + 0.044715 * gate * gate * gate)))


def _proj(x2d, seq, g1, w_in, qg, kg, cos, sin, hq, hk):
    T = x2d.shape[0]
    tm = min(TOKEN_TILE, seq)
    nseq = seq // tm
    row = lambda i: (i, 0)
    const = lambda i: (0, 0)
    pos = lambda i: (i % nseq, 0)
    out_shapes = (
        jax.ShapeDtypeStruct((T, ATTN_W), BF16),
        jax.ShapeDtypeStruct((T, KV_W), BF16),
        jax.ShapeDtypeStruct((T, KV_W), BF16),
        jax.ShapeDtypeStruct((T, KV_W), BF16),
        jax.ShapeDtypeStruct((T, KV_W), BF16),
        jax.ShapeDtypeStruct((T, LRU_W), F32),
        jax.ShapeDtypeStruct((T, LRU_W), F32),
    )
    return pl.pallas_call(
        _proj_kernel,
        grid=(T // tm,),
        in_specs=[
            pl.BlockSpec((tm, D_MODEL), row),
            pl.BlockSpec((1, D_MODEL), const),
            pl.BlockSpec((D_MODEL, IN_W), const),
            pl.BlockSpec((1, ATTN_W), const),
            pl.BlockSpec((1, KV_W), const),
            pl.BlockSpec((tm, LANES), pos),
            pl.BlockSpec((tm, LANES), pos),
            pl.BlockSpec((ATTN_W, ATTN_W), const),
            pl.BlockSpec((KV_W, KV_W), const),
        ],
        out_specs=(
            pl.BlockSpec((tm, ATTN_W), row),
            pl.BlockSpec((tm, KV_W), row),
            pl.BlockSpec((tm, KV_W), row),
            pl.BlockSpec((tm, KV_W), row),
            pl.BlockSpec((tm, KV_W), row),
            pl.BlockSpec((tm, LRU_W), row),
            pl.BlockSpec((tm, LRU_W), row),
        ),
        out_shape=out_shapes,
        compiler_params=_params(("parallel",)),
        name="proj",
    )(x2d, g1, w_in, qg, kg, cos, sin, hq, hk)


def _sigmoid(x):
    return 0.5 * jnp.tanh(0.5 * x) + 0.5


def _scan8(a, b, h_in, first_row, shifts):
    b = b + jnp.where(first_row, a * h_in, 0.0)
    a = jnp.where(first_row, 0.0, a)
    for d in shifts:
        b = b + a * pltpu.roll(b, d, axis=0)
        a = a * pltpu.roll(a, d, axis=0)
    return b


def _lru_kernel(xr_ref, ga_ref, cw_ref, cb_ref, wf_ref, wb_ref, bias_ref, lam_ref, o_ref,
                xpad_s, xc_s, hf_s, hb_s, pre_f, pre_b):
    S = xr_ref.shape[0]
    C = LRU_W
    CH = min(LRU_CHUNK, S)
    NC = S // CH
    G = CH // SUBLANES
    left = CONV_W // 2

    xpad_s[0:SUBLANES, :] = jnp.zeros((SUBLANES, C), F32)
    xpad_s[SUBLANES + S:2 * SUBLANES + S, :] = jnp.zeros((SUBLANES, C), F32)
    xpad_s[SUBLANES:SUBLANES + S, :] = xr_ref[...]
    for c in range(NC):
        u = xpad_s[c * CH:c * CH + CH + 2 * SUBLANES, :]
        acc = cb_ref[...] + jnp.zeros((CH, C), F32)
        for tap in range(CONV_W):
            shift = (left - tap) % (CH + 2 * SUBLANES)
            ush = u if shift == 0 else pltpu.roll(u, shift, axis=0)
            acc = acc + ush[SUBLANES:SUBLANES + CH, :] * cw_ref[tap:tap + 1, :]
        xc_s[c * CH:(c + 1) * CH, :] = acc

    z = -lam_ref[...]
    nsp2 = (-LRU_C * LOG2E) * (jnp.maximum(z, 0.0) + jnp.log1p(jnp.exp(-jnp.abs(z))))
    nsp_f = nsp2[0:1, :]
    nsp_b = nsp2[1:2, :]
    row = lax.broadcasted_iota(I32, (SUBLANES, C), 0)
    is_first = row == 0
    is_last = row == SUBLANES - 1

    def gate_preacts(x_bf, w_ref, b0, pre_ref):
        for g in range(LRU_GROUPS):
            res = jnp.dot(x_bf[:, g * LANES:(g + 1) * LANES], w_ref[g], preferred_element_type=F32)
            pre_ref[:, g * LANES:(g + 1) * LANES] = (
                res[:, :LANES] + bias_ref[:, b0 + g * LANES:b0 + (g + 1) * LANES])
            pre_ref[:, C + g * LANES:C + (g + 1) * LANES] = (
                res[:, LANES:] + bias_ref[:, b0 + C + g * LANES:b0 + C + (g + 1) * LANES])

    def gates(pre_ref, g, x8, nsp_d):
        r = _sigmoid(pre_ref[pl.ds(g * SUBLANES, SUBLANES), 0:C])
        i = _sigmoid(pre_ref[pl.ds(g * SUBLANES, SUBLANES), C:2 * C])
        a = jnp.exp2(nsp_d * r)
        return a, jnp.sqrt(1.0 - a * a) * i * x8

    hf = jnp.zeros((SUBLANES, C), F32)
    hb = jnp.zeros((SUBLANES, C), F32)
    for c in range(NC):
        cf = c
        cb = NC - 1 - c
        gate_preacts(xc_s[cf * CH:(cf + 1) * CH, :].astype(BF16), wf_ref, 0, pre_f)
        gate_preacts(xc_s[cb * CH:(cb + 1) * CH, :].astype(BF16), wb_ref, 2 * C, pre_b)

        def body(g, carry, cf=cf, cb=cb):
            hf, hb = carry
            rf = pl.multiple_of(cf * CH + g * SUBLANES, SUBLANES)
            a, b = gates(pre_f, g, xc_s[pl.ds(rf, SUBLANES), :], nsp_f)
            h = _scan8(a, b, hf, is_first, (1, 2, 4))
            hf_s[pl.ds(rf, SUBLANES), :] = h
            hf = jnp.broadcast_to(h[SUBLANES - 1:SUBLANES, :], (SUBLANES, C))
            gb = G - 1 - g
            rb = pl.multiple_of(cb * CH + gb * SUBLANES, SUBLANES)
            a, b = gates(pre_b, gb, xc_s[pl.ds(rb, SUBLANES), :], nsp_b)
            h = _scan8(a, b, hb, is_last, (SUBLANES - 1, SUBLANES - 2, SUBLANES - 4))
            hb_s[pl.ds(rb, SUBLANES), :] = h
            hb = jnp.broadcast_to(h[0:1, :], (SUBLANES, C))
            return hf, hb

        hf, hb = lax.fori_loop(0, G, body, (hf, hb))

    o_ref[...] = ((hf_s[...] + hb_s[...]) * ga_ref[...]).astype(BF16)


def _lru(xr, ga, cw, cb, wf, wb, bias, lam):
    B, S, C = xr.shape
    CH = min(LRU_CHUNK, S)
    blk = lambda b: (b, 0, 0)
    const2 = lambda b: (0, 0)
    const3 = lambda b: (0, 0, 0)
    return pl.pallas_call(
        _lru_kernel,
        grid=(B,),
        in_specs=[
            pl.BlockSpec((None, S, C), blk),
            pl.BlockSpec((None, S, C), blk),
            pl.BlockSpec((CONV_W, C), const2),
            pl.BlockSpec((1, C), const2),
            pl.BlockSpec((LRU_GROUPS, LANES, 2 * LANES), const3),
            pl.BlockSpec((LRU_GROUPS, LANES, 2 * LANES), const3),
            pl.BlockSpec((1, 4 * C), const2),
            pl.BlockSpec((2, C), const2),
        ],
        out_specs=pl.BlockSpec((None, S, C), blk),
        out_shape=jax.ShapeDtypeStruct((B, S, C), BF16),
        scratch_shapes=[
            pltpu.VMEM((S + 2 * SUBLANES, C), F32),
            pltpu.VMEM((S, C), F32),
            pltpu.VMEM((S, C), F32),
            pltpu.VMEM((S, C), F32),
            pltpu.VMEM((CH, 2 * C), F32),
            pltpu.VMEM((CH, 2 * C), F32),
        ],
        compiler_params=_params(("parallel",)),
        name="lru",
    )(xr, ga, cw, cb, wf, wb, bias, lam)


def _attn_kernel(q_ref, ka_ref, kb_ref, va_ref, vb_ref, o_ref, vxa_s, vxb_s):
    tq = q_ref.shape[0]
    S = ka_ref.shape[0]

    @pl.when(pl.program_id(1) == 0)
    def _():
        ones = jnp.ones((S, LANES), BF16)
        vxa_s[:, :LANES] = va_ref[...]
        vxa_s[:, LANES:] = ones
        vxb_s[:, :LANES] = vb_ref[...]
        vxb_s[:, LANES:] = ones

    lane = lax.broadcasted_iota(I32, (tq, LANES), 1)
    low = lane < HEAD_DIM
    contract_last = (((1,), (1,)), ((), ()))
    for p in range(ATTN_W // LANES):
        kv = (2 * p) // (N_Q_HEADS // N_KV_HEADS)
        qp = q_ref[:, p * LANES:(p + 1) * LANES]
        halves = []
        for half in range(2):
            keep = low if half == 0 else jnp.logical_not(low)
            qm = jnp.where(keep, qp, jnp.zeros_like(qp))
            k_ref = ka_ref if kv == half else kb_ref
            vx_ref = vxa_s if kv == half else vxb_s
            s = lax.dot_general(qm, k_ref[...], contract_last, preferred_element_type=F32)
            sb = s.astype(BF16)
            pr = jnp.exp2(sb - jnp.max(sb, axis=-1, keepdims=True))
            o = jnp.dot(pr, vx_ref[...], preferred_element_type=F32)
            halves.append(o[:, :LANES] / o[:, LANES:])
        o_ref[:, p * LANES:(p + 1) * LANES] = jnp.where(low, halves[0], halves[1]).astype(BF16)


def _attn(q, ka, kb, va, vb):
    B, S, _ = q.shape
    tq = min(Q_TILE, S)
    qblk = lambda b, i: (b, i, 0)
    kblk = lambda b, i: (b, 0, 0)
    return pl.pallas_call(
        _attn_kernel,
        grid=(B, S // tq),
        in_specs=[
            pl.BlockSpec((None, tq, ATTN_W), qblk),
            pl.BlockSpec((None, S, KV_W), kblk),
            pl.BlockSpec((None, S, KV_W), kblk),
            pl.BlockSpec((None, S, KV_W), kblk),
            pl.BlockSpec((None, S, KV_W), kblk),
        ],
        out_specs=pl.BlockSpec((None, tq, ATTN_W), qblk),
        out_shape=jax.ShapeDtypeStruct((B, S, ATTN_W), BF16),
        scratch_shapes=[
            pltpu.VMEM((S, 2 * LANES), BF16),
            pltpu.VMEM((S, 2 * LANES), BF16),
        ],
        compiler_params=_params(("parallel", "arbitrary")),
        name="attn",
    )(q, ka, kb, va, vb)


def _outproj_kernel(a_ref, l_ref, x_ref, wa_ref, wl_ref, g2_ref, rw_ref, x1_ref, h2_ref, aff_ref):
    x1 = (x_ref[...]
          + jnp.dot(a_ref[...], wa_ref[...], preferred_element_type=F32)
          + jnp.dot(l_ref[...], wl_ref[...], preferred_element_type=F32))
    x1_ref[...] = x1
    ms = jnp.mean(x1 * x1, axis=-1, keepdims=True)
    h2 = (x1 * lax.rsqrt(ms + EPS) * g2_ref[...]).astype(BF16)
    h2_ref[...] = h2
    logits = lax.dot_general(rw_ref[...], h2, (((1,), (1,)), ((), ())), preferred_element_type=F32)
    m = jnp.max(logits, axis=0, keepdims=True)
    e = jnp.exp(logits - m)
    aff_ref[...] = e / jnp.sum(e, axis=0, keepdims=True)


def _outproj(attn2d, lru2d, x2d, wa, wl, g2, rwt):
    T = x2d.shape[0]
    tm = min(TOKEN_TILE, T)
    row = lambda i: (i, 0)
    const = lambda i: (0, 0)
    return pl.pallas_call(
        _outproj_kernel,
        grid=(T // tm,),
        in_specs=[
            pl.BlockSpec((tm, ATTN_W), row),
            pl.BlockSpec((tm, LRU_W), row),
            pl.BlockSpec((tm, D_MODEL), row),
            pl.BlockSpec((ATTN_W, D_MODEL), const),
            pl.BlockSpec((LRU_W, D_MODEL), const),
            pl.BlockSpec((1, D_MODEL), const),
            pl.BlockSpec((N_EXPERTS, D_MODEL), const),
        ],
        out_specs=(
            pl.BlockSpec((tm, D_MODEL), row),
            pl.BlockSpec((tm, D_MODEL), row),
            pl.BlockSpec((N_EXPERTS, tm), lambda i: (0, i)),
        ),
        out_shape=(
            jax.ShapeDtypeStruct((T, D_MODEL), F32),
            jax.ShapeDtypeStruct((T, D_MODEL), BF16),
            jax.ShapeDtypeStruct((N_EXPERTS, T), F32),
        ),
        compiler_params=_params(("parallel",)),
        name="outproj",
    )(attn2d, lru2d, x2d, wa, wl, g2, rwt)


def _select_kernel(aff_ref, tri_ref, trinb_ref, gsel_ref, cnt_ref, off_ref, maxn_ref, used_ref,
                   *, cap, T, NB):
    TB = T // NB
    CW = min(2048, T)
    E = N_EXPERTS

    def count_where(pred):
        def body(i, acc):
            bits = pltpu.bitcast(aff_ref[:, pl.ds(pl.multiple_of(i * CW, LANES), CW)], I32)
            m = pred(bits).astype(I32)
            part = m[:, 0:LANES]
            for qd in range(1, CW // LANES):
                part = part + m[:, qd * LANES:(qd + 1) * LANES]
            return acc + part
        acc = lax.fori_loop(0, T // CW, body, jnp.zeros((E, LANES), I32))
        return jnp.sum(acc, axis=1, keepdims=True)

    def bit_step(it, thr):
        cand = thr | jnp.left_shift(jnp.int32(1), 30 - it)
        n = count_where(lambda bits: bits >= cand)
        return jnp.where(n >= cap, cand, thr)

    thr = lax.fori_loop(0, 31, bit_step, jnp.zeros((E, 1), I32))
    n_gt = count_where(lambda bits: bits > thr)
    need = (cap - n_gt).astype(F32)

    lane_nb = lax.broadcasted_iota(I32, (E, NB), 1)

    def blk_step(j, carry):
        eq_before, cnt_acc = carry
        aff = aff_ref[:, pl.ds(pl.multiple_of(j * TB, LANES), TB)]
        bits = pltpu.bitcast(aff, I32)
        gt = bits > thr
        eq = bits == thr
        eqf = jnp.where(eq, 1.0, 0.0)
        pref = jnp.dot(eqf.astype(BF16), tri_ref[...], preferred_element_type=F32)
        take = jnp.logical_and(eq, (eq_before + pref) < need)
        sel = jnp.logical_or(gt, take)
        gsel_ref[:, pl.ds(pl.multiple_of(j * TB, LANES), TB)] = jnp.where(sel, aff, -1.0)
        cnt = jnp.sum(jnp.where(sel, 1.0, 0.0), axis=1, keepdims=True)
        cnt_acc = jnp.where(lane_nb == j, cnt, cnt_acc)
        return eq_before + jnp.sum(eqf, axis=1, keepdims=True), cnt_acc

    _, cnt = lax.fori_loop(0, NB, blk_step, (jnp.zeros((E, 1), F32), jnp.zeros((E, NB), F32)))
    npad = jnp.floor((cnt + (SUBLANES - 1)) * (1.0 / SUBLANES)) * SUBLANES
    off = jnp.dot(npad.astype(BF16), trinb_ref[...], preferred_element_type=F32)
    cnt_ref[...] = cnt.astype(I32)
    off_ref[...] = off.astype(I32)
    maxn_ref[...] = jnp.broadcast_to(jnp.max(cnt, axis=0, keepdims=True), (SUBLANES, NB)).astype(I32)
    used_ref[...] = jnp.broadcast_to(jnp.sum(npad, axis=1, keepdims=True), (E, LANES)).astype(I32)


def _strict_upper(n):
    r = lax.broadcasted_iota(I32, (n, n), 0)
    c = lax.broadcasted_iota(I32, (n, n), 1)
    return (r < c).astype(BF16)


def _select(aff_t, cap, NB):
    E, T = aff_t.shape
    TB = T // NB
    full = lambda shape: pl.BlockSpec(shape, lambda i: (0,) * len(shape))
    return pl.pallas_call(
        functools.partial(_select_kernel, cap=cap, T=T, NB=NB),
        grid=(1,),
        in_specs=[full((E, T)), full((TB, TB)), full((NB, NB))],
        out_specs=(full((E, T)), full((E, NB)), full((E, NB)), full((SUBLANES, NB)), full((E, LANES))),
        out_shape=(
            jax.ShapeDtypeStruct((E, T), F32),
            jax.ShapeDtypeStruct((E, NB), I32),
            jax.ShapeDtypeStruct((E, NB), I32),
            jax.ShapeDtypeStruct((SUBLANES, NB), I32),
            jax.ShapeDtypeStruct((E, LANES), I32),
        ),
        compiler_params=_params(("arbitrary",)),
        name="select",
    )(aff_t, _strict_upper(TB), _strict_upper(NB))


def _pack_pairs(x):
    n = x.shape[1] // 2
    lo = pltpu.bitcast(x[:, :n].astype(BF16).astype(F32), U32)
    hi = pltpu.bitcast(x[:, n:].astype(BF16).astype(F32), U32)
    return hi | (lo >> 16)


def _unpack_pairs(w):
    lo = pltpu.bitcast(w << 16, F32)
    hi = pltpu.bitcast(w & jnp.uint32(0xFFFF0000), F32)
    return jnp.concatenate([lo, hi], axis=1).astype(BF16)


def _onehot_chunk(c, slot, pos, val):
    tokens = pos.shape[1]
    pieces = []
    for p in range(MOE_BLOCK // P_CHUNK):
        row0 = c * MOE_BLOCK + p * P_CHUNK
        e, s0 = row0 // slot, row0 % slot
        s = (s0 + lax.broadcasted_iota(I32, (P_CHUNK, tokens), 0)).astype(F32)
        pieces.append(jnp.where(pos[e:e + 1, :] == s, val[e:e + 1, :], 0.0))
    return jnp.concatenate(pieces, axis=0).astype(BF16)


def _slot_copies(j, NB, off_s, slot, buf, lst, sem, to_list):
    copies = []
    for e in range(N_EXPERTS):
        b = buf.at[e * slot:(e + 1) * slot]
        l = lst.at[e, pl.ds(pl.multiple_of(off_s[e * NB + j], SUBLANES), slot)]
        copies.append(pltpu.make_async_copy(b, l, sem) if to_list else pltpu.make_async_copy(l, b, sem))
    return copies


def _picks(gsel_ref, tri_ref):
    g = gsel_ref[...]
    sel = g >= 0.0
    self_ = jnp.where(sel, 1.0, 0.0)
    pos = jnp.dot(self_.astype(BF16), tri_ref[...], preferred_element_type=F32)
    return self_, jnp.where(sel, g, 0.0), pos


def _gather_kernel(off_s, maxn_s, used_s, h2_ref, gsel_ref, tri_ref, list_ref, out_buf, zero_buf, sem,
                   *, NB, cap):
    j = pl.program_id(0)
    TB = h2_ref.shape[0]
    L = list_ref.shape[1]
    self_, _, pos = _picks(gsel_ref, tri_ref)

    def step(slot):
        h2 = h2_ref[...]
        per_dot = GATHER_ROWS // MOE_BLOCK
        for d in range(N_EXPERTS * slot // GATHER_ROWS):
            onehot = jnp.concatenate(
                [_onehot_chunk(d * per_dot + c, slot, pos, self_) for c in range(per_dot)], axis=0)
            rows = jnp.dot(onehot, h2, preferred_element_type=F32)
            out_buf[d * GATHER_ROWS:(d + 1) * GATHER_ROWS, :] = _pack_pairs(rows)
        copies = _slot_copies(j, NB, off_s, slot, out_buf, list_ref, sem, True)
        for cp in copies:
            cp.start()
        for cp in copies:
            cp.wait()

    lax.cond(maxn_s[j] <= SLOT_SMALL, lambda: step(SLOT_SMALL), lambda: step(TB))

    @pl.when(j == NB - 1)
    def _():
        zero_buf[...] = jnp.zeros(zero_buf.shape, U32)
        for r in range(-(-(L - cap) // ZERO_ROWS)):
            copies = []
            for e in range(N_EXPERTS):
                start = jnp.minimum(used_s[e] + r * ZERO_ROWS, L - ZERO_ROWS)
                dst = list_ref.at[e, pl.ds(pl.multiple_of(start, SUBLANES), ZERO_ROWS)]
                copies.append(pltpu.make_async_copy(zero_buf, dst, sem))
            for cp in copies:
                cp.start()
            for cp in copies:
                cp.wait()


def _gather(tables, h2, gsel, L, cap):
    off, maxn, used = tables
    T = h2.shape[0]
    TB = min(MOE_BLOCK, T)
    NB = T // TB
    grid_spec = pltpu.PrefetchScalarGridSpec(
        num_scalar_prefetch=3,
        grid=(NB,),
        in_specs=[
            pl.BlockSpec((TB, D_MODEL), lambda j, *_: (j, 0)),
            pl.BlockSpec((N_EXPERTS, TB), lambda j, *_: (0, j)),
            pl.BlockSpec((TB, TB), lambda j, *_: (0, 0)),
        ],
        out_specs=pl.BlockSpec(memory_space=pl.ANY),
        scratch_shapes=[
            pltpu.VMEM((N_EXPERTS * TB, PACKED_W), U32),
            pltpu.VMEM((ZERO_ROWS, PACKED_W), U32),
            pltpu.SemaphoreType.DMA,
        ],
    )
    return pl.pallas_call(
        functools.partial(_gather_kernel, NB=NB, cap=cap),
        grid_spec=grid_spec,
        out_shape=jax.ShapeDtypeStruct((N_EXPERTS, L, PACKED_W), U32),
        compiler_params=_params(("arbitrary",)),
        name="gather",
    )(off, maxn, used, h2, gsel, _strict_upper(TB))


def _experts_kernel(used_s, xs_ref, wg_ref, wu_ref, wd_ref, ys_ref):
    e = pl.program_id(0)
    c = pl.program_id(1)
    live = c * EXPERT_TILE < used_s[e] + MOE_BLOCK

    @pl.when(live)
    def _():
        xs = _unpack_pairs(xs_ref[...])
        hg = jnp.dot(xs, wg_ref[...], preferred_element_type=F32)
        hu = jnp.dot(xs, wu_ref[...], preferred_element_type=F32)
        hid = (hg * _sigmoid(hg) * hu).astype(BF16)
        ys_ref[...] = _pack_pairs(jnp.dot(hid, wd_ref[...], preferred_element_type=F32))

    @pl.when(jnp.logical_not(live))
    def _():
        ys_ref[...] = jnp.zeros(ys_ref.shape, U32)


def _experts(used, xs, wg, wu, wd):
    E, L, W = xs.shape
    D = wg.shape[1]
    NT = L // EXPERT_TILE

    def in_tile(e, c, used):
        last = (used[e] + MOE_BLOCK + (EXPERT_TILE - 1)) // EXPERT_TILE - 1
        return (e, jnp.minimum(c, last), 0)

    wblk = lambda e, c, used: (e, 0, 0)
    grid_spec = pltpu.PrefetchScalarGridSpec(
        num_scalar_prefetch=1,
        grid=(E, NT),
        in_specs=[
            pl.BlockSpec((None, EXPERT_TILE, W), in_tile),
            pl.BlockSpec((None, D, D), wblk),
            pl.BlockSpec((None, D, D), wblk),
            pl.BlockSpec((None, D, D), wblk),
        ],
        out_specs=pl.BlockSpec((None, EXPERT_TILE, W), lambda e, c, used: (e, c, 0)),
    )
    return pl.pallas_call(
        _experts_kernel,
        grid_spec=grid_spec,
        out_shape=jax.ShapeDtypeStruct((E, L, W), U32),
        compiler_params=_params(("arbitrary", "arbitrary")),
        name="experts",
    )(used, xs, wg, wu, wd)


def _combine_kernel(off_s, maxn_s, x1_ref, gsel_ref, tri_ref, ys_ref, y_ref, in_buf, sem, *, NB):
    j = pl.program_id(0)
    TB = x1_ref.shape[0]
    _, gate, pos = _picks(gsel_ref, tri_ref)

    def step(slot):
        copies = _slot_copies(j, NB, off_s, slot, in_buf, ys_ref, sem, False)
        for cp in copies:
            cp.start()
        for cp in copies:
            cp.wait()
        acc = x1_ref[...]
        for c in range(N_EXPERTS * slot // MOE_BLOCK):
            pc = _onehot_chunk(c, slot, pos, gate)
            yc = _unpack_pairs(in_buf[c * MOE_BLOCK:(c + 1) * MOE_BLOCK, :])
            acc = acc + lax.dot_general(pc, yc, (((0,), (0,)), ((), ())), preferred_element_type=F32)
        y_ref[...] = acc

    lax.cond(maxn_s[j] <= SLOT_SMALL, lambda: step(SLOT_SMALL), lambda: step(TB))


def _combine(tables, x1, gsel, ys):
    off, maxn, _ = tables
    T = x1.shape[0]
    TB = min(MOE_BLOCK, T)
    NB = T // TB
    grid_spec = pltpu.PrefetchScalarGridSpec(
        num_scalar_prefetch=2,
        grid=(NB,),
        in_specs=[
            pl.BlockSpec((TB, D_MODEL), lambda j, *_: (j, 0)),
            pl.BlockSpec((N_EXPERTS, TB), lambda j, *_: (0, j)),
            pl.BlockSpec((TB, TB), lambda j, *_: (0, 0)),
            pl.BlockSpec(memory_space=pl.ANY),
        ],
        out_specs=pl.BlockSpec((TB, D_MODEL), lambda j, *_: (j, 0)),
        scratch_shapes=[
            pltpu.VMEM((N_EXPERTS * TB, PACKED_W), U32),
            pltpu.SemaphoreType.DMA,
        ],
    )
    return pl.pallas_call(
        functools.partial(_combine_kernel, NB=NB),
        grid_spec=grid_spec,
        out_shape=jax.ShapeDtypeStruct((T, D_MODEL), F32),
        compiler_params=_params(("arbitrary",)),
        name="combine",
    )(off, maxn, x1, gsel, _strict_upper(TB), ys)


def _rope_tables(seq):
    pos = jnp.arange(seq, dtype=I32)
    row = (pos // GRID_W).astype(F32)
    col = (pos % GRID_W).astype(F32)
    axis_dim = HEAD_DIM // 2
    inv_freq = ROPE_THETA ** (-jnp.arange(0, axis_dim, 2, dtype=F32) / axis_dim)
    ra = row[:, None] * inv_freq[None, :]
    ca = col[:, None] * inv_freq[None, :]
    cos = jnp.concatenate([jnp.cos(ra), jnp.cos(ra), jnp.cos(ca), jnp.cos(ca)], axis=-1)
    sin = jnp.concatenate([-jnp.sin(ra), jnp.sin(ra), -jnp.sin(ca), jnp.sin(ca)], axis=-1)
    reps = LANES // HEAD_DIM
    return jnp.tile(cos, (1, reps)), jnp.tile(sin, (1, reps))


def _head_mean_matrix(width):
    r = lax.broadcasted_iota(I32, (width, width), 0) // HEAD_DIM
    c = lax.broadcasted_iota(I32, (width, width), 1) // HEAD_DIM
    return jnp.where(r == c, 1.0 / HEAD_DIM, 0.0).astype(BF16)


def _block_diag(w):
    n, d, _ = w.shape
    eye = jnp.eye(n, dtype=w.dtype)
    return jnp.einsum('nde,nm->ndme', w, eye).reshape(n * d, n * d)


def _group_gate_weights(w_r, w_i):
    per = LANES // LRU_BLOCK_DIM
    groups = [jnp.concatenate([_block_diag(w_r[g * per:(g + 1) * per]),
                               _block_diag(w_i[g * per:(g + 1) * per])], axis=1)
              for g in range(LRU_GROUPS)]
    return jnp.stack(groups).astype(BF16)


def _layer(x, weights):
    (g1, w_in, qg, kg, cw, cb, wf, wb, bias, lam, wa, wl, g2, rwt, wg, wu, wd) = weights
    B, S, D = x.shape
    T = B * S
    x2d = x.reshape(T, D)
    cos, sin = _rope_tables(S)
    hq = _head_mean_matrix(ATTN_W)
    hk = _head_mean_matrix(KV_W)
    q, ka, kb, va, vb, xr, ga = _proj(x2d, S, g1, w_in, qg, kg, cos, sin, hq, hk)
    lru = _lru(xr.reshape(B, S, LRU_W), ga.reshape(B, S, LRU_W), cw, cb, wf, wb, bias, lam)
    att = _attn(q.reshape(B, S, ATTN_W), ka.reshape(B, S, KV_W), kb.reshape(B, S, KV_W),
                va.reshape(B, S, KV_W), vb.reshape(B, S, KV_W))
    x1, h2, aff_t = _outproj(att.reshape(T, ATTN_W), lru.reshape(T, LRU_W), x2d, wa, wl, g2, rwt)

    cap = max(1, EXPERT_CAPACITY_FACTOR * T // N_EXPERTS)
    TB = min(MOE_BLOCK, T)
    NB = T // TB
    L = -(-(cap + SUBLANES * NB + MOE_BLOCK) // EXPERT_TILE) * EXPERT_TILE
    gsel, cnt, off, maxn, used = _select(aff_t, cap, NB)
    tables = (off.reshape(-1), maxn[0], used[:, 0])
    xs = _gather(tables, h2, gsel, L, cap)
    ys = _experts(tables[2], xs, wg, wu, wd)
    y = _combine(tables, x1, gsel, ys)
    return y.reshape(B, S, D)


def kernel(x_prompt, x_sample, norm1_g, w_in, q_norm_g, k_norm_g, conv_w, conv_b, lru_w_r, lru_b_r,
           lru_w_i, lru_b_i, lru_lambda, w_out, norm2_g, router_w, exp_w_gate, exp_w_up, exp_w_down):
    layers = []
    for l in range(norm1_g.shape[0]):
        bias = jnp.concatenate([lru_b_r[l, 0], lru_b_i[l, 0], lru_b_r[l, 1], lru_b_i[l, 1]])[None, :]
        layers.append((
            norm1_g[l][None, :], w_in[l].astype(BF16),
            jnp.tile(q_norm_g[l], N_Q_HEADS)[None, :], jnp.tile(k_norm_g[l], N_KV_HEADS)[None, :],
            conv_w[l], conv_b[l][None, :],
            _group_gate_weights(lru_w_r[l, 0], lru_w_i[l, 0]), _group_gate_weights(lru_w_r[l, 1], lru_w_i[l, 1]),
            bias, lru_lambda[l],
            w_out[l, :ATTN_W].astype(BF16), w_out[l, ATTN_W:].astype(BF16),
            norm2_g[l][None, :], router_w[l].T.astype(BF16),
            exp_w_gate[l].astype(BF16), exp_w_up[l].astype(BF16), exp_w_down[l].astype(BF16),
        ))
    ys = []
    for x in (x_prompt, x_sample):
        for weights in layers:
            x = _layer(x, weights)
        ys.append(x)
    return tuple(ys)
```

```python
import functools

import jax
import jax.numpy as jnp
from jax import lax
from jax.experimental import pallas as pl
from jax.experimental.pallas import tpu as pltpu

F32 = jnp.float32
BF16 = jnp.bfloat16
I32 = jnp.int32
U32 = jnp.uint32

D_MODEL = 1024
GRID_W = 64
HEAD_DIM = 64
N_Q_HEADS = 8
N_KV_HEADS = 2
ATTN_W = N_Q_HEADS * HEAD_DIM
KV_W = N_KV_HEADS * HEAD_DIM
LRU_W = D_MODEL - ATTN_W
LRU_BLOCKS = 8
LRU_BLOCK_DIM = LRU_W // LRU_BLOCKS
CONV_W = 4
LRU_C = 8.0
N_EXPERTS = 16
EXPERT_CAPACITY_FACTOR = 2
ROPE_THETA = 10000.0
EPS = 1e-6
IN_W = ATTN_W + 2 * KV_W + 2 * LRU_W
LOG2E = 1.4426950408889634

LANES = 128
SUBLANES = 8
VMEM_LIMIT = 56 * 1024 * 1024

TOKEN_TILE = 1024
Q_TILE = 512
LRU_CHUNK = 512
LRU_GROUPS = LRU_W // LANES
MOE_BLOCK = 256
GATHER_ROWS = 1024
EXPERT_TILE = 512
ZERO_ROWS = 256
SLOT_SMALL = 64
P_CHUNK = 32
PACKED_W = D_MODEL // 2


def _params(sem, vmem=VMEM_LIMIT):
    return pltpu.CompilerParams(dimension_semantics=sem, vmem_limit_bytes=vmem)


def _swap16(x, lane):
    fwd = pltpu.roll(x, 16, axis=1)
    bwd = pltpu.roll(x, LANES - 16, axis=1)
    return jnp.where((lane & 16) == 0, bwd, fwd)


def _split_dot(x, w):
    hi = x.astype(BF16)
    lo = (x - hi.astype(F32)).astype(BF16)
    return (jnp.dot(hi, w, preferred_element_type=F32)
            + jnp.dot(lo, w, preferred_element_type=F32))


def _proj_kernel(x_ref, g1_ref, w_ref, qg_ref, kg_ref, cos_ref, sin_ref, hq_ref, hk_ref,
                 q_ref, ka_ref, kb_ref, va_ref, vb_ref, xr_ref, ga_ref):
    x = x_ref[...]
    ms = jnp.mean(x * x, axis=-1, keepdims=True)
    h = (x * lax.rsqrt(ms + EPS) * g1_ref[...]).astype(BF16)
    proj = jnp.dot(h, w_ref[...], preferred_element_type=F32)
    q = proj[:, :ATTN_W]
    k = proj[:, ATTN_W:ATTN_W + KV_W]
    v = proj[:, ATTN_W + KV_W:ATTN_W + 2 * KV_W]
    xr = proj[:, ATTN_W + 2 * KV_W:ATTN_W + 2 * KV_W + LRU_W]
    gate = proj[:, ATTN_W + 2 * KV_W + LRU_W:]

    cos = cos_ref[...]
    sin = sin_ref[...]
    lane = lax.broadcasted_iota(I32, cos.shape, 1)

    qn = q * lax.rsqrt(_split_dot(q * q, hq_ref[...]) + EPS) * qg_ref[...]
    kn = k * lax.rsqrt(_split_dot(k * k, hk_ref[...]) + EPS) * kg_ref[...]

    qscale = (HEAD_DIM ** -0.5) * LOG2E
    for p in range(ATTN_W // LANES):
        qc = qn[:, p * LANES:(p + 1) * LANES]
        qr = qc * cos + _swap16(qc, lane) * sin
        q_ref[:, p * LANES:(p + 1) * LANES] = (qr * qscale).astype(BF16)
    kr = kn * cos + _swap16(kn, lane) * sin
    ka_ref[...] = kr.astype(BF16)
    kb_ref[...] = pltpu.roll(kr, HEAD_DIM, axis=1).astype(BF16)
    va_ref[...] = v.astype(BF16)
    vb_ref[...] = pltpu.roll(v, HEAD_DIM, axis=1).astype(BF16)
    xr_ref[...] = xr
    ga_ref[...] = 0.5 * gate * (1.0 + jnp.tanh(0.7978845608028654 * (gate + 0.044715 * gate * gate * gate)))


def _proj(x2d, seq, g1, w_in, qg, kg, cos, sin, hq, hk):
    T = x2d.shape[0]
    tm = min(TOKEN_TILE, seq)
    nseq = seq // tm
    row = lambda i: (i, 0)
    const = lambda i: (0, 0)
    pos = lambda i: (i % nseq, 0)
    out_shapes = (
        jax.ShapeDtypeStruct((T, ATTN_W), BF16),
        jax.ShapeDtypeStruct((T, KV_W), BF16),
        jax.ShapeDtypeStruct((T, KV_W), BF16),
        jax.ShapeDtypeStruct((T, KV_W), BF16),
        jax.ShapeDtypeStruct((T, KV_W), BF16),
        jax.ShapeDtypeStruct((T, LRU_W), F32),
        jax.ShapeDtypeStruct((T, LRU_W), F32),
    )
    return pl.pallas_call(
        _proj_kernel,
        grid=(T // tm,),
        in_specs=[
            pl.BlockSpec((tm, D_MODEL), row),
            pl.BlockSpec((1, D_MODEL), const),
            pl.BlockSpec((D_MODEL, IN_W), const),
            pl.BlockSpec((1, ATTN_W), const),
            pl.BlockSpec((1, KV_W), const),
            pl.BlockSpec((tm, LANES), pos),
            pl.BlockSpec((tm, LANES), pos),
            pl.BlockSpec((ATTN_W, ATTN_W), const),
            pl.BlockSpec((KV_W, KV_W), const),
        ],
        out_specs=(
            pl.BlockSpec((tm, ATTN_W), row),
            pl.BlockSpec((tm, KV_W), row),
            pl.BlockSpec((tm, KV_W), row),
            pl.BlockSpec((tm, KV_W), row),
            pl.BlockSpec((tm, KV_W), row),
            pl.BlockSpec((tm, LRU_W), row),
            pl.BlockSpec((tm, LRU_W), row),
        ),
        out_shape=out_shapes,
        compiler_params=_params(("parallel",)),
        name="proj",
    )(x2d, g1, w_in, qg, kg, cos, sin, hq, hk)


def _sigmoid(x):
    return 0.5 * jnp.tanh(0.5 * x) + 0.5


def _scan8(a, b, h_in, first_row, shifts):
    b = b + jnp.where(first_row, a * h_in, 0.0)
    a = jnp.where(first_row, 0.0, a)
    for d in shifts:
        b = b + a * pltpu.roll(b, d, axis=0)
        a = a * pltpu.roll(a, d, axis=0)
    return b


def _lru_kernel(xr_ref, ga_ref, cw_ref, cb_ref, wf_ref, wb_ref, bias_ref, lam_ref, o_ref,
                xpad_s, xc_s, hf_s, hb_s, pre_f, pre_b):
    S = xr_ref.shape[0]
    C = LRU_W
    CH = min(LRU_CHUNK, S)
    NC = S // CH
    G = CH // SUBLANES
    left = CONV_W // 2

    xpad_s[0:SUBLANES, :] = jnp.zeros((SUBLANES, C), F32)
    xpad_s[SUBLANES + S:2 * SUBLANES + S, :] = jnp.zeros((SUBLANES, C), F32)
    xpad_s[SUBLANES:SUBLANES + S, :] = xr_ref[...]
    cw_half = 0.5 * cw_ref[...]
    cb_half = 0.5 * cb_ref[...]
    for c in range(NC):
        u = xpad_s[c * CH:c * CH + CH + 2 * SUBLANES, :]
        acc = cb_half + jnp.zeros((CH, C), F32)
        for tap in range(CONV_W):
            shift = (left - tap) % (CH + 2 * SUBLANES)
            ush = u if shift == 0 else pltpu.roll(u, shift, axis=0)
            acc = acc + ush[SUBLANES:SUBLANES + CH, :] * cw_half[tap:tap + 1, :]
        xc_s[c * CH:(c + 1) * CH, :] = acc

    z = -lam_ref[...]
    c1 = (-0.5 * LRU_C * LOG2E) * (jnp.maximum(z, 0.0) + jnp.log1p(jnp.exp(-jnp.abs(z))))
    c1_f = c1[0:1, :]
    c1_b = c1[1:2, :]
    bias_half = 0.5 * bias_ref[...]
    row = lax.broadcasted_iota(I32, (SUBLANES, C), 0)
    is_first = row == 0
    is_last = row == SUBLANES - 1

    def gate_preacts(x_bf, w_ref, b0, pre_ref):
        for g in range(LRU_GROUPS):
            res = jnp.dot(x_bf[:, g * LANES:(g + 1) * LANES], w_ref[g], preferred_element_type=F32)
            pre_ref[:, g * LANES:(g + 1) * LANES] = (
                res[:, :LANES] + bias_half[:, b0 + g * LANES:b0 + (g + 1) * LANES])
            pre_ref[:, C + g * LANES:C + (g + 1) * LANES] = (
                res[:, LANES:] + bias_half[:, b0 + C + g * LANES:b0 + C + (g + 1) * LANES])

    def gates(pre_ref, g, x8_half, c1_d):
        tr = jnp.tanh(pre_ref[pl.ds(g * SUBLANES, SUBLANES), 0:C])
        ti = jnp.tanh(pre_ref[pl.ds(g * SUBLANES, SUBLANES), C:2 * C])
        a = jnp.exp2(c1_d * tr + c1_d)
        return a, jnp.sqrt(1.0 - a * a) * ((ti + 1.0) * x8_half)

    hf = jnp.zeros((SUBLANES, C), F32)
    hb = jnp.zeros((SUBLANES, C), F32)
    for c in range(NC):
        cf = c
        cb = NC - 1 - c
        gate_preacts(xc_s[cf * CH:(cf + 1) * CH, :].astype(BF16), wf_ref, 0, pre_f)
        gate_preacts(xc_s[cb * CH:(cb + 1) * CH, :].astype(BF16), wb_ref, 2 * C, pre_b)

        def body(g, carry, cf=cf, cb=cb):
            hf, hb = carry
            rf = pl.multiple_of(cf * CH + g * SUBLANES, SUBLANES)
            a, b = gates(pre_f, g, xc_s[pl.ds(rf, SUBLANES), :], c1_f)
            h = _scan8(a, b, hf, is_first, (1, 2, 4))
            hf_s[pl.ds(rf, SUBLANES), :] = h
            hf = jnp.broadcast_to(h[SUBLANES - 1:SUBLANES, :], (SUBLANES, C))
            gb = G - 1 - g
            rb = pl.multiple_of(cb * CH + gb * SUBLANES, SUBLANES)
            a, b = gates(pre_b, gb, xc_s[pl.ds(rb, SUBLANES), :], c1_b)
            h = _scan8(a, b, hb, is_last, (SUBLANES - 1, SUBLANES - 2, SUBLANES - 4))
            hb_s[pl.ds(rb, SUBLANES), :] = h
            hb = jnp.broadcast_to(h[0:1, :], (SUBLANES, C))
            return hf, hb

        hf, hb = lax.fori_loop(0, G, body, (hf, hb), unroll=2)

    o_ref[...] = ((hf_s[...] + hb_s[...]) * ga_ref[...]).astype(BF16)


def _lru(xr, ga, cw, cb, wf, wb, bias, lam):
    B, S, C = xr.shape
    CH = min(LRU_CHUNK, S)
    blk = lambda b: (b, 0, 0)
    const2 = lambda b: (0, 0)
    const3 = lambda b: (0, 0, 0)
    return pl.pallas_call(
        _lru_kernel,
        grid=(B,),
        in_specs=[
            pl.BlockSpec((None, S, C), blk),
            pl.BlockSpec((None, S, C), blk),
            pl.BlockSpec((CONV_W, C), const2),
            pl.BlockSpec((1, C), const2),
            pl.BlockSpec((LRU_GROUPS, LANES, 2 * LANES), const3),
            pl.BlockSpec((LRU_GROUPS, LANES, 2 * LANES), const3),
            pl.BlockSpec((1, 4 * C), const2),
            pl.BlockSpec((2, C), const2),
        ],
        out_specs=pl.BlockSpec((None, S, C), blk),
        out_shape=jax.ShapeDtypeStruct((B, S, C), BF16),
        scratch_shapes=[
            pltpu.VMEM((S + 2 * SUBLANES, C), F32),
            pltpu.VMEM((S, C), F32),
            pltpu.VMEM((S, C), F32),
            pltpu.VMEM((S, C), F32),
            pltpu.VMEM((CH, 2 * C), F32),
            pltpu.VMEM((CH, 2 * C), F32),
        ],
        compiler_params=_params(("parallel",)),
        name="lru",
    )(xr, ga, cw, cb, wf, wb, bias, lam)


def _attn_kernel(q_ref, ka_ref, kb_ref, va_ref, vb_ref, o_ref, vxa_s, vxb_s):
    tq = q_ref.shape[0]
    S = ka_ref.shape[0]

    @pl.when(pl.program_id(1) == 0)
    def _():
        ones = jnp.ones((S, LANES), BF16)
        vxa_s[:, :LANES] = va_ref[...]
        vxa_s[:, LANES:] = ones
        vxb_s[:, :LANES] = vb_ref[...]
        vxb_s[:, LANES:] = ones

    lane = lax.broadcasted_iota(I32, (tq, LANES), 1)
    low = lane < HEAD_DIM
    contract_last = (((1,), (1,)), ((), ()))
    for p in range(ATTN_W // LANES):
        kv = (2 * p) // (N_Q_HEADS // N_KV_HEADS)
        qp = q_ref[:, p * LANES:(p + 1) * LANES]
        halves = []
        for half in range(2):
            keep = low if half == 0 else jnp.logical_not(low)
            qm = jnp.where(keep, qp, jnp.zeros_like(qp))
            k_ref = ka_ref if kv == half else kb_ref
            vx_ref = vxa_s if kv == half else vxb_s
            s = lax.dot_general(qm, k_ref[...], contract_last, preferred_element_type=F32)
            sb = s.astype(BF16)
            pr = jnp.exp2(sb - jnp.max(sb, axis=-1, keepdims=True))
            o = jnp.dot(pr, vx_ref[...], preferred_element_type=F32)
            halves.append(o[:, :LANES] / o[:, LANES:])
        o_ref[:, p * LANES:(p + 1) * LANES] = jnp.where(low, halves[0], halves[1]).astype(BF16)


def _attn(q, ka, kb, va, vb):
    B, S, _ = q.shape
    tq = min(Q_TILE, S)
    qblk = lambda b, i: (b, i, 0)
    kblk = lambda b, i: (b, 0, 0)
    return pl.pallas_call(
        _attn_kernel,
        grid=(B, S // tq),
        in_specs=[
            pl.BlockSpec((None, tq, ATTN_W), qblk),
            pl.BlockSpec((None, S, KV_W), kblk),
            pl.BlockSpec((None, S, KV_W), kblk),
            pl.BlockSpec((None, S, KV_W), kblk),
            pl.BlockSpec((None, S, KV_W), kblk),
        ],
        out_specs=pl.BlockSpec((None, tq, ATTN_W), qblk),
        out_shape=jax.ShapeDtypeStruct((B, S, ATTN_W), BF16),
        scratch_shapes=[
            pltpu.VMEM((S, 2 * LANES), BF16),
            pltpu.VMEM((S, 2 * LANES), BF16),
        ],
        compiler_params=_params(("parallel", "arbitrary")),
        name="attn",
    )(q, ka, kb, va, vb)


def _outproj_kernel(a_ref, l_ref, x_ref, wa_ref, wl_ref, g2_ref, rw_ref, x1_ref, h2_ref, aff_ref):
    x1 = (x_ref[...]
          + jnp.dot(a_ref[...], wa_ref[...], preferred_element_type=F32)
          + jnp.dot(l_ref[...], wl_ref[...], preferred_element_type=F32))
    x1_ref[...] = x1
    ms = jnp.mean(x1 * x1, axis=-1, keepdims=True)
    h2 = (x1 * lax.rsqrt(ms + EPS) * g2_ref[...]).astype(BF16)
    h2_ref[...] = h2
    logits = lax.dot_general(rw_ref[...], h2, (((1,), (1,)), ((), ())), preferred_element_type=F32)
    m = jnp.max(logits, axis=0, keepdims=True)
    e = jnp.exp(logits - m)
    aff_ref[...] = e / jnp.sum(e, axis=0, keepdims=True)


def _outproj(attn2d, lru2d, x2d, wa, wl, g2, rwt):
    T = x2d.shape[0]
    tm = min(TOKEN_TILE, T)
    row = lambda i: (i, 0)
    const = lambda i: (0, 0)
    return pl.pallas_call(
        _outproj_kernel,
        grid=(T // tm,),
        in_specs=[
            pl.BlockSpec((tm, ATTN_W), row),
            pl.BlockSpec((tm, LRU_W), row),
            pl.BlockSpec((tm, D_MODEL), row),
            pl.BlockSpec((ATTN_W, D_MODEL), const),
            pl.BlockSpec((LRU_W, D_MODEL), const),
            pl.BlockSpec((1, D_MODEL), const),
            pl.BlockSpec((N_EXPERTS, D_MODEL), const),
        ],
        out_specs=(
            pl.BlockSpec((tm, D_MODEL), row),
            pl.BlockSpec((tm, D_MODEL), row),
            pl.BlockSpec((N_EXPERTS, tm), lambda i: (0, i)),
        ),
        out_shape=(
            jax.ShapeDtypeStruct((T, D_MODEL), F32),
            jax.ShapeDtypeStruct((T, D_MODEL), BF16),
            jax.ShapeDtypeStruct((N_EXPERTS, T), F32),
        ),
        compiler_params=_params(("parallel",)),
        name="outproj",
    )(attn2d, lru2d, x2d, wa, wl, g2, rwt)


def _select_kernel(aff_ref, tri_ref, trinb_ref, gsel_ref, cnt_ref, off_ref, maxn_ref, used_ref,
                   *, cap, T, NB):
    TB = T // NB
    CW = min(2048, T)
    E = N_EXPERTS

    def count_where(pred):
        def body(i, acc):
            bits = pltpu.bitcast(aff_ref[:, pl.ds(pl.multiple_of(i * CW, LANES), CW)], I32)
            m = pred(bits).astype(I32)
            part = m[:, 0:LANES]
            for qd in range(1, CW // LANES):
                part = part + m[:, qd * LANES:(qd + 1) * LANES]
            return acc + part
        acc = lax.fori_loop(0, T // CW, body, jnp.zeros((E, LANES), I32))
        return jnp.sum(acc, axis=1, keepdims=True)

    def bit_step(it, thr):
        cand = thr | jnp.left_shift(jnp.int32(1), 30 - it)
        n = count_where(lambda bits: bits >= cand)
        return jnp.where(n >= cap, cand, thr)

    thr = lax.fori_loop(0, 31, bit_step, jnp.zeros((E, 1), I32))
    n_gt = count_where(lambda bits: bits > thr)
    need = (cap - n_gt).astype(F32)

    lane_nb = lax.broadcasted_iota(I32, (E, NB), 1)

    def blk_step(j, carry):
        eq_before, cnt_acc = carry
        aff = aff_ref[:, pl.ds(pl.multiple_of(j * TB, LANES), TB)]
        bits = pltpu.bitcast(aff, I32)
        gt = bits > thr
        eq = bits == thr
        eqf = jnp.where(eq, 1.0, 0.0)
        pref = jnp.dot(eqf.astype(BF16), tri_ref[...], preferred_element_type=F32)
        take = jnp.logical_and(eq, (eq_before + pref) < need)
        sel = jnp.logical_or(gt, take)
        gsel_ref[:, pl.ds(pl.multiple_of(j * TB, LANES), TB)] = jnp.where(sel, aff, -1.0)
        cnt = jnp.sum(jnp.where(sel, 1.0, 0.0), axis=1, keepdims=True)
        cnt_acc = jnp.where(lane_nb == j, cnt, cnt_acc)
        return eq_before + jnp.sum(eqf, axis=1, keepdims=True), cnt_acc

    _, cnt = lax.fori_loop(0, NB, blk_step, (jnp.zeros((E, 1), F32), jnp.zeros((E, NB), F32)))
    npad = jnp.floor((cnt + (SUBLANES - 1)) * (1.0 / SUBLANES)) * SUBLANES
    off = jnp.dot(npad.astype(BF16), trinb_ref[...], preferred_element_type=F32)
    cnt_ref[...] = cnt.astype(I32)
    off_ref[...] = off.astype(I32)
    maxn_ref[...] = jnp.broadcast_to(jnp.max(cnt, axis=0, keepdims=True), (SUBLANES, NB)).astype(I32)
    used_ref[...] = jnp.broadcast_to(jnp.sum(npad, axis=1, keepdims=True), (E, LANES)).astype(I32)


def _strict_upper(n):
    r = lax.broadcasted_iota(I32, (n, n), 0)
    c = lax.broadcasted_iota(I32, (n, n), 1)
    return (r < c).astype(BF16)


def _select(aff_t, cap, NB):
    E, T = aff_t.shape
    TB = T // NB
    full = lambda shape: pl.BlockSpec(shape, lambda i: (0,) * len(shape))
    return pl.pallas_call(
        functools.partial(_select_kernel, cap=cap, T=T, NB=NB),
        grid=(1,),
        in_specs=[full((E, T)), full((TB, TB)), full((NB, NB))],
        out_specs=(full((E, T)), full((E, NB)), full((E, NB)), full((SUBLANES, NB)), full((E, LANES))),
        out_shape=(
            jax.ShapeDtypeStruct((E, T), F32),
            jax.ShapeDtypeStruct((E, NB), I32),
            jax.ShapeDtypeStruct((E, NB), I32),
            jax.ShapeDtypeStruct((SUBLANES, NB), I32),
            jax.ShapeDtypeStruct((E, LANES), I32),
        ),
        compiler_params=_params(("arbitrary",)),
        name="select",
    )(aff_t, _strict_upper(TB), _strict_upper(NB))


def _pack_pairs(x):
    n = x.shape[1] // 2
    lo = pltpu.bitcast(x[:, :n].astype(BF16).astype(F32), U32)
    hi = pltpu.bitcast(x[:, n:].astype(BF16).astype(F32), U32)
    return hi | (lo >> 16)


def _unpack_pairs(w):
    lo = pltpu.bitcast(w << 16, F32)
    hi = pltpu.bitcast(w & jnp.uint32(0xFFFF0000), F32)
    return jnp.concatenate([lo, hi], axis=1).astype(BF16)


def _onehot_chunk(c, slot, pos, val):
    tokens = pos.shape[1]
    pieces = []
    for p in range(MOE_BLOCK // P_CHUNK):
        row0 = c * MOE_BLOCK + p * P_CHUNK
        e, s0 = row0 // slot, row0 % slot
        s = (s0 + lax.broadcasted_iota(I32, (P_CHUNK, tokens), 0)).astype(F32)
        pieces.append(jnp.where(pos[e:e + 1, :] == s, val[e:e + 1, :], 0.0))
    return jnp.concatenate(pieces, axis=0).astype(BF16)


def _slot_copies(j, NB, off_s, slot, buf, lst, sem, to_list):
    copies = []
    for e in range(N_EXPERTS):
        b = buf.at[e * slot:(e + 1) * slot]
        l = lst.at[e, pl.ds(pl.multiple_of(off_s[e * NB + j], SUBLANES), slot)]
        copies.append(pltpu.make_async_copy(b, l, sem) if to_list else pltpu.make_async_copy(l, b, sem))
    return copies


def _picks(gsel_ref, tri_ref):
    g = gsel_ref[...]
    sel = g >= 0.0
    self_ = jnp.where(sel, 1.0, 0.0)
    pos = jnp.dot(self_.astype(BF16), tri_ref[...], preferred_element_type=F32)
    return self_, jnp.where(sel, g, 0.0), pos


def _on_slot(maxn_s, jj, big_slot, fn):
    lax.cond(maxn_s[jj] <= SLOT_SMALL, lambda: fn(SLOT_SMALL), lambda: fn(big_slot))


def _gather_kernel(off_s, maxn_s, used_s, h2_ref, gsel_ref, tri_ref, list_ref, out_buf, zero_buf, sems,
                   *, NB, cap):
    j = pl.program_id(0)
    TB = h2_ref.shape[0]
    L = list_ref.shape[1]
    self_, _, pos = _picks(gsel_ref, tri_ref)
    buf = out_buf.at[j % 2]

    def compute(slot):
        h2 = h2_ref[...]
        per_dot = GATHER_ROWS // MOE_BLOCK
        for d in range(N_EXPERTS * slot // GATHER_ROWS):
            onehot = jnp.concatenate(
                [_onehot_chunk(d * per_dot + c, slot, pos, self_) for c in range(per_dot)], axis=0)
            rows = jnp.dot(onehot, h2, preferred_element_type=F32)
            buf[d * GATHER_ROWS:(d + 1) * GATHER_ROWS, :] = _pack_pairs(rows)

    def start_copies(jj):
        def fn(slot):
            for cp in _slot_copies(jj, NB, off_s, slot, out_buf.at[jj % 2], list_ref, sems.at[jj % 2], True):
                cp.start()
        _on_slot(maxn_s, jj, TB, fn)

    def wait_copies(jj):
        def fn(slot):
            for cp in _slot_copies(jj, NB, off_s, slot, out_buf.at[jj % 2], list_ref, sems.at[jj % 2], True):
                cp.wait()
        _on_slot(maxn_s, jj, TB, fn)

    _on_slot(maxn_s, j, TB, compute)

    @pl.when(j > 0)
    def _():
        wait_copies(jnp.maximum(j - 1, 0))

    start_copies(j)

    @pl.when(j == NB - 1)
    def _():
        wait_copies(j)
        zero_buf[...] = jnp.zeros(zero_buf.shape, U32)
        sem = sems.at[0]
        for r in range(-(-(L - cap) // ZERO_ROWS)):
            copies = []
            for e in range(N_EXPERTS):
                start = jnp.minimum(used_s[e] + r * ZERO_ROWS, L - ZERO_ROWS)
                dst = list_ref.at[e, pl.ds(pl.multiple_of(start, SUBLANES), ZERO_ROWS)]
                copies.append(pltpu.make_async_copy(zero_buf, dst, sem))
            for cp in copies:
                cp.start()
            for cp in copies:
                cp.wait()


def _gather(tables, h2, gsel, L, cap):
    off, maxn, used = tables
    T = h2.shape[0]
    TB = min(MOE_BLOCK, T)
    NB = T // TB
    grid_spec = pltpu.PrefetchScalarGridSpec(
        num_scalar_prefetch=3,
        grid=(NB,),
        in_specs=[
            pl.BlockSpec((TB, D_MODEL), lambda j, *_: (j, 0)),
            pl.BlockSpec((N_EXPERTS, TB), lambda j, *_: (0, j)),
            pl.BlockSpec((TB, TB), lambda j, *_: (0, 0)),
        ],
        out_specs=pl.BlockSpec(memory_space=pl.ANY),
        scratch_shapes=[
            pltpu.VMEM((2, N_EXPERTS * TB, PACKED_W), U32),
            pltpu.VMEM((ZERO_ROWS, PACKED_W), U32),
            pltpu.SemaphoreType.DMA((2,)),
        ],
    )
    return pl.pallas_call(
        functools.partial(_gather_kernel, NB=NB, cap=cap),
        grid_spec=grid_spec,
        out_shape=jax.ShapeDtypeStruct((N_EXPERTS, L, PACKED_W), U32),
        compiler_params=_params(("arbitrary",)),
        name="gather",
    )(off, maxn, used, h2, gsel, _strict_upper(TB))


def _experts_kernel(used_s, xs_ref, wg_ref, wu_ref, wd_ref, ys_ref, wg_s, wu_s, wd_s):
    e = pl.program_id(0)
    c = pl.program_id(1)
    live = c * EXPERT_TILE < used_s[e] + MOE_BLOCK

    @pl.when(c == 0)
    def _():
        wg_s[...] = wg_ref[...].astype(BF16)
        wu_s[...] = wu_ref[...].astype(BF16)
        wd_s[...] = wd_ref[...].astype(BF16)

    @pl.when(live)
    def _():
        xs = _unpack_pairs(xs_ref[...])
        hg = jnp.dot(xs, wg_s[...], preferred_element_type=F32)
        hu = jnp.dot(xs, wu_s[...], preferred_element_type=F32)
        hid = (hg * _sigmoid(hg) * hu).astype(BF16)
        ys_ref[...] = _pack_pairs(jnp.dot(hid, wd_s[...], preferred_element_type=F32))

    @pl.when(jnp.logical_not(live))
    def _():
        ys_ref[...] = jnp.zeros(ys_ref.shape, U32)


def _experts(used, xs, wg, wu, wd):
    E, L, W = xs.shape
    D = wg.shape[1]
    NT = L // EXPERT_TILE

    def in_tile(e, c, used):
        last = (used[e] + MOE_BLOCK + (EXPERT_TILE - 1)) // EXPERT_TILE - 1
        return (e, jnp.minimum(c, last), 0)

    wblk = lambda e, c, used: (e, 0, 0)
    grid_spec = pltpu.PrefetchScalarGridSpec(
        num_scalar_prefetch=1,
        grid=(E, NT),
        in_specs=[
            pl.BlockSpec((None, EXPERT_TILE, W), in_tile),
            pl.BlockSpec((None, D, D), wblk),
            pl.BlockSpec((None, D, D), wblk),
            pl.BlockSpec((None, D, D), wblk),
        ],
        out_specs=pl.BlockSpec((None, EXPERT_TILE, W), lambda e, c, used: (e, c, 0)),
        scratch_shapes=[pltpu.VMEM((D, D), BF16), pltpu.VMEM((D, D), BF16), pltpu.VMEM((D, D), BF16)],
    )
    return pl.pallas_call(
        _experts_kernel,
        grid_spec=grid_spec,
        out_shape=jax.ShapeDtypeStruct((E, L, W), U32),
        compiler_params=_params(("arbitrary", "arbitrary")),
        name="experts",
    )(used, xs, wg, wu, wd)


def _combine_kernel(off_s, maxn_s, x1_ref, gsel_ref, tri_ref, ys_ref, y_ref, in_buf, sems, *, NB):
    j = pl.program_id(0)
    TB = x1_ref.shape[0]
    buf = in_buf.at[j % 2]

    def start_copies(jj):
        def fn(slot):
            for cp in _slot_copies(jj, NB, off_s, slot, in_buf.at[jj % 2], ys_ref, sems.at[jj % 2], False):
                cp.start()
        _on_slot(maxn_s, jj, TB, fn)

    def wait_copies(jj):
        def fn(slot):
            for cp in _slot_copies(jj, NB, off_s, slot, in_buf.at[jj % 2], ys_ref, sems.at[jj % 2], False):
                cp.wait()
        _on_slot(maxn_s, jj, TB, fn)

    @pl.when(j == 0)
    def _():
        start_copies(j)

    @pl.when(j + 1 < NB)
    def _():
        start_copies(jnp.minimum(j + 1, NB - 1))

    _, gate, pos = _picks(gsel_ref, tri_ref)
    wait_copies(j)

    def compute(slot):
        acc = x1_ref[...]
        for c in range(N_EXPERTS * slot // MOE_BLOCK):
            pc = _onehot_chunk(c, slot, pos, gate)
            yc = _unpack_pairs(buf[c * MOE_BLOCK:(c + 1) * MOE_BLOCK, :])
            acc = acc + lax.dot_general(pc, yc, (((0,), (0,)), ((), ())), preferred_element_type=F32)
        y_ref[...] = acc

    _on_slot(maxn_s, j, TB, compute)


def _combine(tables, x1, gsel, ys):
    off, maxn, _ = tables
    T = x1.shape[0]
    TB = min(MOE_BLOCK, T)
    NB = T // TB
    grid_spec = pltpu.PrefetchScalarGridSpec(
        num_scalar_prefetch=2,
        grid=(NB,),
        in_specs=[
            pl.BlockSpec((TB, D_MODEL), lambda j, *_: (j, 0)),
            pl.BlockSpec((N_EXPERTS, TB), lambda j, *_: (0, j)),
            pl.BlockSpec((TB, TB), lambda j, *_: (0, 0)),
            pl.BlockSpec(memory_space=pl.ANY),
        ],
        out_specs=pl.BlockSpec((TB, D_MODEL), lambda j, *_: (j, 0)),
        scratch_shapes=[
            pltpu.VMEM((2, N_EXPERTS * TB, PACKED_W), U32),
            pltpu.SemaphoreType.DMA((2,)),
        ],
    )
    return pl.pallas_call(
        functools.partial(_combine_kernel, NB=NB),
        grid_spec=grid_spec,
        out_shape=jax.ShapeDtypeStruct((T, D_MODEL), F32),
        compiler_params=_params(("arbitrary",)),
        name="combine",
    )(off, maxn, x1, gsel, _strict_upper(TB), ys)


def _rope_tables(seq):
    pos = jnp.arange(seq, dtype=I32)
    row = (pos // GRID_W).astype(F32)
    col = (pos % GRID_W).astype(F32)
    axis_dim = HEAD_DIM // 2
    inv_freq = ROPE_THETA ** (-jnp.arange(0, axis_dim, 2, dtype=F32) / axis_dim)
    ra = row[:, None] * inv_freq[None, :]
    ca = col[:, None] * inv_freq[None, :]
    cos = jnp.concatenate([jnp.cos(ra), jnp.cos(ra), jnp.cos(ca), jnp.cos(ca)], axis=-1)
    sin = jnp.concatenate([-jnp.sin(ra), jnp.sin(ra), -jnp.sin(ca), jnp.sin(ca)], axis=-1)
    reps = LANES // HEAD_DIM
    return jnp.tile(cos, (1, reps)), jnp.tile(sin, (1, reps))


def _head_mean_matrix(width):
    r = lax.broadcasted_iota(I32, (width, width), 0) // HEAD_DIM
    c = lax.broadcasted_iota(I32, (width, width), 1) // HEAD_DIM
    return jnp.where(r == c, 1.0 / HEAD_DIM, 0.0).astype(BF16)


def _block_diag(w):
    n, d, _ = w.shape
    eye = jnp.eye(n, dtype=w.dtype)
    return jnp.einsum('nde,nm->ndme', w, eye).reshape(n * d, n * d)


def _group_gate_weights(w_r, w_i):
    per = LANES // LRU_BLOCK_DIM
    groups = [jnp.concatenate([_block_diag(w_r[g * per:(g + 1) * per]),
                               _block_diag(w_i[g * per:(g + 1) * per])], axis=1)
              for g in range(LRU_GROUPS)]
    return jnp.stack(groups).astype(BF16)


def _layer(x, weights):
    (g1, w_in, qg, kg, cw, cb, wf, wb, bias, lam, wa, wl, g2, rwt, wg, wu, wd) = weights
    B, S, D = x.shape
    T = B * S
    x2d = x.reshape(T, D)
    cos, sin = _rope_tables(S)
    hq = _head_mean_matrix(ATTN_W)
    hk = _head_mean_matrix(KV_W)
    q, ka, kb, va, vb, xr, ga = _proj(x2d, S, g1, w_in, qg, kg, cos, sin, hq, hk)
    lru = _lru(xr.reshape(B, S, LRU_W), ga.reshape(B, S, LRU_W), cw, cb, wf, wb, bias, lam)
    att = _attn(q.reshape(B, S, ATTN_W), ka.reshape(B, S, KV_W), kb.reshape(B, S, KV_W),
                va.reshape(B, S, KV_W), vb.reshape(B, S, KV_W))
    x1, h2, aff_t = _outproj(att.reshape(T, ATTN_W), lru.reshape(T, LRU_W), x2d, wa, wl, g2, rwt)

    cap = max(1, EXPERT_CAPACITY_FACTOR * T // N_EXPERTS)
    TB = min(MOE_BLOCK, T)
    NB = T // TB
    L = -(-(cap + SUBLANES * NB + MOE_BLOCK) // EXPERT_TILE) * EXPERT_TILE
    gsel, cnt, off, maxn, used = _select(aff_t, cap, NB)
    tables = (off.reshape(-1), maxn[0], used[:, 0])
    xs = _gather(tables, h2, gsel, L, cap)
    ys = _experts(tables[2], xs, wg, wu, wd)
    y = _combine(tables, x1, gsel, ys)
    return y.reshape(B, S, D)


def kernel(x_prompt, x_sample, norm1_g, w_in, q_norm_g, k_norm_g, conv_w, conv_b, lru_w_r, lru_b_r,
           lru_w_i, lru_b_i, lru_lambda, w_out, norm2_g, router_w, exp_w_gate, exp_w_up, exp_w_down):
    layers = []
    for l in range(norm1_g.shape[0]):
        bias = jnp.concatenate([lru_b_r[l, 0], lru_b_i[l, 0], lru_b_r[l, 1], lru_b_i[l, 1]])[None, :]
        layers.append((
            norm1_g[l][None, :], w_in[l].astype(BF16),
            jnp.tile(q_norm_g[l], N_Q_HEADS)[None, :], jnp.tile(k_norm_g[l], N_KV_HEADS)[None, :],
            conv_w[l], conv_b[l][None, :],
            _group_gate_weights(lru_w_r[l, 0], lru_w_i[l, 0]), _group_gate_weights(lru_w_r[l, 1], lru_w_i[l, 1]),
            bias, lru_lambda[l],
            w_out[l, :ATTN_W].astype(BF16), w_out[l, ATTN_W:].astype(BF16),
            norm2_g[l][None, :], router_w[l].T.astype(BF16),
            exp_w_gate[l], exp_w_up[l], exp_w_down[l],
        ))
    ys = []
    for x in (x_prompt, x_sample):
        for weights in layers:
            x = _layer(x, weights)
        ys.append(x)
    return tuple(ys)
```

```python
import functools

import jax
import jax.numpy as jnp
from jax import lax
from jax.experimental import pallas as pl
from jax.experimental.pallas import tpu as pltpu

F32 = jnp.float32
BF16 = jnp.bfloat16
I32 = jnp.int32
U32 = jnp.uint32

D_MODEL = 1024
GRID_W = 64
HEAD_DIM = 64
N_Q_HEADS = 8
N_KV_HEADS = 2
ATTN_W = N_Q_HEADS * HEAD_DIM
KV_W = N_KV_HEADS * HEAD_DIM
LRU_W = D_MODEL - ATTN_W
LRU_BLOCKS = 8
LRU_BLOCK_DIM = LRU_W // LRU_BLOCKS
CONV_W = 4
LRU_C = 8.0
N_EXPERTS = 16
EXPERT_CAPACITY_FACTOR = 2
ROPE_THETA = 10000.0
EPS = 1e-6
IN_W = ATTN_W + 2 * KV_W + 2 * LRU_W
LOG2E = 1.4426950408889634

LANES = 128
SUBLANES = 8
VMEM_LIMIT = 56 * 1024 * 1024

TOKEN_TILE = 1024
Q_TILE = 512
LRU_CHUNK = 512
LRU_GROUPS = LRU_W // LANES
MOE_BLOCK = 256
GATHER_ROWS = 1024
EXPERT_TILE = 512
ZERO_ROWS = 256
SLOT_SIZES = (48, 64)
P_CHUNK = 16
PACKED_W = D_MODEL // 2


def _params(sem, vmem=VMEM_LIMIT):
    return pltpu.CompilerParams(dimension_semantics=sem, vmem_limit_bytes=vmem)


def _swap16(x, lane):
    fwd = pltpu.roll(x, 16, axis=1)
    bwd = pltpu.roll(x, LANES - 16, axis=1)
    return jnp.where((lane & 16) == 0, bwd, fwd)


def _split_dot(x, w):
    hi = x.astype(BF16)
    lo = (x - hi.astype(F32)).astype(BF16)
    return (jnp.dot(hi, w, preferred_element_type=F32)
            + jnp.dot(lo, w, preferred_element_type=F32))


def _proj_kernel(x_ref, g1_ref, w_ref, qg_ref, kg_ref, cos_ref, sin_ref, hq_ref, hk_ref,
                 q_ref, ka_ref, kb_ref, v_ref, xr_ref, ga_ref):
    x = x_ref[...]
    ms = jnp.mean(x * x, axis=-1, keepdims=True)
    h = (x * lax.rsqrt(ms + EPS) * g1_ref[...]).astype(BF16)
    proj = jnp.dot(h, w_ref[...], preferred_element_type=F32)
    q = proj[:, :ATTN_W]
    k = proj[:, ATTN_W:ATTN_W + KV_W]
    v = proj[:, ATTN_W + KV_W:ATTN_W + 2 * KV_W]
    xr = proj[:, ATTN_W + 2 * KV_W:ATTN_W + 2 * KV_W + LRU_W]
    gate = proj[:, ATTN_W + 2 * KV_W + LRU_W:]

    cos = cos_ref[...]
    sin = sin_ref[...]
    lane = lax.broadcasted_iota(I32, cos.shape, 1)

    qn = q * lax.rsqrt(_split_dot(q * q, hq_ref[...]) + EPS) * qg_ref[...]
    kn = k * lax.rsqrt(_split_dot(k * k, hk_ref[...]) + EPS) * kg_ref[...]

    qscale = (HEAD_DIM ** -0.5) * LOG2E
    for p in range(ATTN_W // LANES):
        qc = qn[:, p * LANES:(p + 1) * LANES]
        qr = qc * cos + _swap16(qc, lane) * sin
        q_ref[:, p * LANES:(p + 1) * LANES] = (qr * qscale).astype(BF16)
    kr = kn * cos + _swap16(kn, lane) * sin
    ka_ref[...] = kr.astype(BF16)
    kb_ref[...] = pltpu.roll(kr, HEAD_DIM, axis=1).astype(BF16)
    v_ref[...] = v.astype(BF16)
    xr_ref[...] = xr
    ga_ref[...] = 0.5 * gate * (1.0 + jnp.tanh(0.7978845608028654 * (gate + 0.044715 * gate * gate * gate)))


def _proj(x2d, seq, g1, w_in, qg, kg, cos, sin, hq, hk):
    T = x2d.shape[0]
    tm = min(TOKEN_TILE, seq)
    nseq = seq // tm
    row = lambda i: (i, 0)
    const = lambda i: (0, 0)
    pos = lambda i: (i % nseq, 0)
    out_shapes = (
        jax.ShapeDtypeStruct((T, ATTN_W), BF16),
        jax.ShapeDtypeStruct((T, KV_W), BF16),
        jax.ShapeDtypeStruct((T, KV_W), BF16),
        jax.ShapeDtypeStruct((T, KV_W), BF16),
        jax.ShapeDtypeStruct((T, LRU_W), F32),
        jax.ShapeDtypeStruct((T, LRU_W), F32),
    )
    return pl.pallas_call(
        _proj_kernel,
        grid=(T // tm,),
        in_specs=[
            pl.BlockSpec((tm, D_MODEL), row),
            pl.BlockSpec((1, D_MODEL), const),
            pl.BlockSpec((D_MODEL, IN_W), const),
            pl.BlockSpec((1, ATTN_W), const),
            pl.BlockSpec((1, KV_W), const),
            pl.BlockSpec((tm, LANES), pos),
            pl.BlockSpec((tm, LANES), pos),
            pl.BlockSpec((ATTN_W, ATTN_W), const),
            pl.BlockSpec((KV_W, KV_W), const),
        ],
        out_specs=(
            pl.BlockSpec((tm, ATTN_W), row),
            pl.BlockSpec((tm, KV_W), row),
            pl.BlockSpec((tm, KV_W), row),
            pl.BlockSpec((tm, KV_W), row),
            pl.BlockSpec((tm, LRU_W), row),
            pl.BlockSpec((tm, LRU_W), row),
        ),
        out_shape=out_shapes,
        compiler_params=_params(("parallel",)),
        name="proj",
    )(x2d, g1, w_in, qg, kg, cos, sin, hq, hk)


def _sigmoid(x):
    return 0.5 * jnp.tanh(0.5 * x) + 0.5


def _scan8(a, b, h_in, first_row, shifts):
    b = b + jnp.where(first_row, a * h_in, 0.0)
    a = jnp.where(first_row, 0.0, a)
    for d in shifts:
        b = b + a * pltpu.roll(b, d, axis=0)
        a = a * pltpu.roll(a, d, axis=0)
    return b


def _lru_kernel(xr_ref, ga_ref, cw_ref, cb_ref, wf_ref, wb_ref, bias_ref, lam_ref, o_ref,
                xpad_s, xc_s, hf_s, hb_s, pre_f, pre_b):
    S = xr_ref.shape[0]
    C = LRU_W
    CH = min(LRU_CHUNK, S)
    NC = S // CH
    G = CH // SUBLANES
    left = CONV_W // 2

    xpad_s[0:SUBLANES, :] = jnp.zeros((SUBLANES, C), F32)
    xpad_s[SUBLANES + S:2 * SUBLANES + S, :] = jnp.zeros((SUBLANES, C), F32)
    xpad_s[SUBLANES:SUBLANES + S, :] = xr_ref[...]
    cw_half = 0.5 * cw_ref[...]
    cb_half = 0.5 * cb_ref[...]
    for c in range(NC):
        u = xpad_s[c * CH:c * CH + CH + 2 * SUBLANES, :]
        acc = cb_half + jnp.zeros((CH, C), F32)
        for tap in range(CONV_W):
            shift = (left - tap) % (CH + 2 * SUBLANES)
            ush = u if shift == 0 else pltpu.roll(u, shift, axis=0)
            acc = acc + ush[SUBLANES:SUBLANES + CH, :] * cw_half[tap:tap + 1, :]
        xc_s[c * CH:(c + 1) * CH, :] = acc

    z = -lam_ref[...]
    c1 = (-0.5 * LRU_C * LOG2E) * (jnp.maximum(z, 0.0) + jnp.log1p(jnp.exp(-jnp.abs(z))))
    c1_f = c1[0:1, :]
    c1_b = c1[1:2, :]
    bias_half = 0.5 * bias_ref[...]
    row = lax.broadcasted_iota(I32, (SUBLANES, C), 0)
    is_first = row == 0
    is_last = row == SUBLANES - 1

    def gate_preacts(x_bf, w_ref, b0, pre_ref):
        for g in range(LRU_GROUPS):
            res = jnp.dot(x_bf[:, g * LANES:(g + 1) * LANES], w_ref[g], preferred_element_type=F32)
            pre_ref[:, g * LANES:(g + 1) * LANES] = (
                res[:, :LANES] + bias_half[:, b0 + g * LANES:b0 + (g + 1) * LANES])
            pre_ref[:, C + g * LANES:C + (g + 1) * LANES] = (
                res[:, LANES:] + bias_half[:, b0 + C + g * LANES:b0 + C + (g + 1) * LANES])

    def gates(pre_ref, g, x8_half, c1_d):
        tr = jnp.tanh(pre_ref[pl.ds(g * SUBLANES, SUBLANES), 0:C])
        ti = jnp.tanh(pre_ref[pl.ds(g * SUBLANES, SUBLANES), C:2 * C])
        a = jnp.exp2(c1_d * tr + c1_d)
        return a, jnp.sqrt(1.0 - a * a) * ((ti + 1.0) * x8_half)

    hf = jnp.zeros((SUBLANES, C), F32)
    hb = jnp.zeros((SUBLANES, C), F32)
    for c in range(NC):
        cf = c
        cb = NC - 1 - c
        gate_preacts(xc_s[cf * CH:(cf + 1) * CH, :].astype(BF16), wf_ref, 0, pre_f)
        gate_preacts(xc_s[cb * CH:(cb + 1) * CH, :].astype(BF16), wb_ref, 2 * C, pre_b)

        def body(g, carry, cf=cf, cb=cb):
            hf, hb = carry
            rf = pl.multiple_of(cf * CH + g * SUBLANES, SUBLANES)
            a, b = gates(pre_f, g, xc_s[pl.ds(rf, SUBLANES), :], c1_f)
            h = _scan8(a, b, hf, is_first, (1, 2, 4))
            hf_s[pl.ds(rf, SUBLANES), :] = h
            hf = jnp.broadcast_to(h[SUBLANES - 1:SUBLANES, :], (SUBLANES, C))
            gb = G - 1 - g
            rb = pl.multiple_of(cb * CH + gb * SUBLANES, SUBLANES)
            a, b = gates(pre_b, gb, xc_s[pl.ds(rb, SUBLANES), :], c1_b)
            h = _scan8(a, b, hb, is_last, (SUBLANES - 1, SUBLANES - 2, SUBLANES - 4))
            hb_s[pl.ds(rb, SUBLANES), :] = h
            hb = jnp.broadcast_to(h[0:1, :], (SUBLANES, C))
            return hf, hb

        hf, hb = lax.fori_loop(0, G, body, (hf, hb), unroll=2)

    o_ref[...] = ((hf_s[...] + hb_s[...]) * ga_ref[...]).astype(BF16)


def _lru(xr, ga, cw, cb, wf, wb, bias, lam):
    B, S, C = xr.shape
    CH = min(LRU_CHUNK, S)
    blk = lambda b: (b, 0, 0)
    const2 = lambda b: (0, 0)
    const3 = lambda b: (0, 0, 0)
    return pl.pallas_call(
        _lru_kernel,
        grid=(B,),
        in_specs=[
            pl.BlockSpec((None, S, C), blk),
            pl.BlockSpec((None, S, C), blk),
            pl.BlockSpec((CONV_W, C), const2),
            pl.BlockSpec((1, C), const2),
            pl.BlockSpec((LRU_GROUPS, LANES, 2 * LANES), const3),
            pl.BlockSpec((LRU_GROUPS, LANES, 2 * LANES), const3),
            pl.BlockSpec((1, 4 * C), const2),
            pl.BlockSpec((2, C), const2),
        ],
        out_specs=pl.BlockSpec((None, S, C), blk),
        out_shape=jax.ShapeDtypeStruct((B, S, C), BF16),
        scratch_shapes=[
            pltpu.VMEM((S + 2 * SUBLANES, C), F32),
            pltpu.VMEM((S, C), F32),
            pltpu.VMEM((S, C), F32),
            pltpu.VMEM((S, C), F32),
            pltpu.VMEM((CH, 2 * C), F32),
            pltpu.VMEM((CH, 2 * C), F32),
        ],
        compiler_params=_params(("parallel",)),
        name="lru",
    )(xr, ga, cw, cb, wf, wb, bias, lam)


ONES_ROWS = 16


def _attn_kernel(q_ref, ka_ref, kb_ref, v_ref, o_ref, vt_s):
    tq = q_ref.shape[0]
    S = ka_ref.shape[0]

    @pl.when(pl.program_id(1) == 0)
    def _():
        vt_s[0:KV_W, :] = v_ref[...].astype(F32).T.astype(BF16)
        vt_s[KV_W:KV_W + ONES_ROWS, :] = jnp.ones((ONES_ROWS, S), BF16)

    lane = lax.broadcasted_iota(I32, (tq, LANES), 1)
    low = lane < HEAD_DIM
    contract_last = (((1,), (1,)), ((), ()))
    q_per_kv = N_Q_HEADS // N_KV_HEADS
    heads = [(p, half) for p in range(ATTN_W // LANES) for half in range(2)]

    def scores(p, half):
        qp = q_ref[:, p * LANES:(p + 1) * LANES]
        qm = jnp.where(low if half == 0 else jnp.logical_not(low), qp, jnp.zeros_like(qp))
        k_ref = ka_ref if (2 * p) // q_per_kv == half else kb_ref
        st = lax.dot_general(k_ref[...], qm, contract_last, preferred_element_type=F32)
        return st.astype(BF16)

    halves = []
    ahead = [scores(*heads[0]), scores(*heads[1])]
    for i, (p, half) in enumerate(heads):
        sb = ahead.pop(0)
        if i + 2 < len(heads):
            ahead.append(scores(*heads[i + 2]))
        kv = (2 * p) // q_per_kv
        pt = jnp.exp2(sb - jnp.max(sb, axis=0, keepdims=True))
        ot = jnp.dot(vt_s[...], pt, preferred_element_type=F32)
        halves.append(ot[kv * HEAD_DIM:(kv + 1) * HEAD_DIM, :] / ot[KV_W:KV_W + 1, :])
        if half == 1:
            pair_t = jnp.concatenate(halves, axis=0)
            o_ref[:, p * LANES:(p + 1) * LANES] = pair_t.T.astype(BF16)
            halves = []


def _attn(q, ka, kb, v):
    B, S, _ = q.shape
    tq = min(Q_TILE, S)
    qblk = lambda b, i: (b, i, 0)
    kblk = lambda b, i: (b, 0, 0)
    return pl.pallas_call(
        _attn_kernel,
        grid=(B, S // tq),
        in_specs=[
            pl.BlockSpec((None, tq, ATTN_W), qblk),
            pl.BlockSpec((None, S, KV_W), kblk),
            pl.BlockSpec((None, S, KV_W), kblk),
            pl.BlockSpec((None, S, KV_W), kblk),
        ],
        out_specs=pl.BlockSpec((None, tq, ATTN_W), qblk),
        out_shape=jax.ShapeDtypeStruct((B, S, ATTN_W), BF16),
        scratch_shapes=[pltpu.VMEM((KV_W + ONES_ROWS, S), BF16)],
        compiler_params=_params(("parallel", "arbitrary")),
        name="attn",
    )(q, ka, kb, v)


def _outproj_kernel(a_ref, l_ref, x_ref, wa_ref, wl_ref, g2_ref, rw_ref, x1_ref, h2_ref, aff_ref):
    x1 = (x_ref[...]
          + jnp.dot(a_ref[...], wa_ref[...], preferred_element_type=F32)
          + jnp.dot(l_ref[...], wl_ref[...], preferred_element_type=F32))
    x1_ref[...] = x1
    ms = jnp.mean(x1 * x1, axis=-1, keepdims=True)
    h2 = (x1 * lax.rsqrt(ms + EPS) * g2_ref[...]).astype(BF16)
    h2_ref[...] = h2
    logits = lax.dot_general(rw_ref[...], h2, (((1,), (1,)), ((), ())), preferred_element_type=F32)
    m = jnp.max(logits, axis=0, keepdims=True)
    e = jnp.exp(logits - m)
    aff_ref[...] = e / jnp.sum(e, axis=0, keepdims=True)


def _outproj(attn2d, lru2d, x2d, wa, wl, g2, rwt):
    T = x2d.shape[0]
    tm = min(TOKEN_TILE, T)
    row = lambda i: (i, 0)
    const = lambda i: (0, 0)
    return pl.pallas_call(
        _outproj_kernel,
        grid=(T // tm,),
        in_specs=[
            pl.BlockSpec((tm, ATTN_W), row),
            pl.BlockSpec((tm, LRU_W), row),
            pl.BlockSpec((tm, D_MODEL), row),
            pl.BlockSpec((ATTN_W, D_MODEL), const),
            pl.BlockSpec((LRU_W, D_MODEL), const),
            pl.BlockSpec((1, D_MODEL), const),
            pl.BlockSpec((N_EXPERTS, D_MODEL), const),
        ],
        out_specs=(
            pl.BlockSpec((tm, D_MODEL), row),
            pl.BlockSpec((tm, D_MODEL), row),
            pl.BlockSpec((N_EXPERTS, tm), lambda i: (0, i)),
        ),
        out_shape=(
            jax.ShapeDtypeStruct((T, D_MODEL), F32),
            jax.ShapeDtypeStruct((T, D_MODEL), BF16),
            jax.ShapeDtypeStruct((N_EXPERTS, T), F32),
        ),
        compiler_params=_params(("parallel",)),
        name="outproj",
    )(attn2d, lru2d, x2d, wa, wl, g2, rwt)


def _select_kernel(aff_ref, tri_ref, trinb_ref, gsel_ref, cnt_ref, off_ref, maxn_ref, *, cap, T, NB):
    TB = T // NB
    CW = min(2048, T)
    E = N_EXPERTS

    def count_where(pred):
        def body(i, acc):
            bits = pltpu.bitcast(aff_ref[:, pl.ds(pl.multiple_of(i * CW, LANES), CW)], I32)
            m = pred(bits).astype(I32)
            part = m[:, 0:LANES]
            for qd in range(1, CW // LANES):
                part = part + m[:, qd * LANES:(qd + 1) * LANES]
            return acc + part
        acc = lax.fori_loop(0, T // CW, body, jnp.zeros((E, LANES), I32))
        return jnp.sum(acc, axis=1, keepdims=True)

    def bit_step(it, thr):
        cand = thr | jnp.left_shift(jnp.int32(1), 30 - it)
        n = count_where(lambda bits: bits >= cand)
        return jnp.where(n >= cap, cand, thr)

    thr = lax.fori_loop(0, 31, bit_step, jnp.zeros((E, 1), I32))
    n_gt = count_where(lambda bits: bits > thr)
    need = (cap - n_gt).astype(F32)

    lane_nb = lax.broadcasted_iota(I32, (E, NB), 1)

    def blk_step(j, carry):
        eq_before, cnt_acc = carry
        aff = aff_ref[:, pl.ds(pl.multiple_of(j * TB, LANES), TB)]
        bits = pltpu.bitcast(aff, I32)
        gt = bits > thr
        eq = bits == thr
        eqf = jnp.where(eq, 1.0, 0.0)
        pref = jnp.dot(eqf.astype(BF16), tri_ref[...], preferred_element_type=F32)
        take = jnp.logical_and(eq, (eq_before + pref) < need)
        sel = jnp.logical_or(gt, take)
        gsel_ref[:, pl.ds(pl.multiple_of(j * TB, LANES), TB)] = jnp.where(sel, aff, -1.0)
        cnt = jnp.sum(jnp.where(sel, 1.0, 0.0), axis=1, keepdims=True)
        cnt_acc = jnp.where(lane_nb == j, cnt, cnt_acc)
        return eq_before + jnp.sum(eqf, axis=1, keepdims=True), cnt_acc

    _, cnt = lax.fori_loop(0, NB, blk_step, (jnp.zeros((E, 1), F32), jnp.zeros((E, NB), F32)))
    off = jnp.dot(cnt.astype(BF16), trinb_ref[...], preferred_element_type=F32)
    cnt_ref[...] = cnt.astype(I32)
    off_ref[...] = off.astype(I32)
    maxn_ref[...] = jnp.broadcast_to(jnp.max(cnt, axis=0, keepdims=True), (SUBLANES, NB)).astype(I32)


def _strict_upper(n):
    r = lax.broadcasted_iota(I32, (n, n), 0)
    c = lax.broadcasted_iota(I32, (n, n), 1)
    return (r < c).astype(BF16)


def _select(aff_t, cap, NB):
    E, T = aff_t.shape
    TB = T // NB
    full = lambda shape: pl.BlockSpec(shape, lambda i: (0,) * len(shape))
    return pl.pallas_call(
        functools.partial(_select_kernel, cap=cap, T=T, NB=NB),
        grid=(1,),
        in_specs=[full((E, T)), full((TB, TB)), full((NB, NB))],
        out_specs=(full((E, T)), full((E, NB)), full((E, NB)), full((SUBLANES, NB))),
        out_shape=(
            jax.ShapeDtypeStruct((E, T), F32),
            jax.ShapeDtypeStruct((E, NB), I32),
            jax.ShapeDtypeStruct((E, NB), I32),
            jax.ShapeDtypeStruct((SUBLANES, NB), I32),
        ),
        compiler_params=_params(("arbitrary",)),
        name="select",
    )(aff_t, _strict_upper(TB), _strict_upper(NB))


def _pack_pairs(x):
    n = x.shape[1] // 2
    lo = pltpu.bitcast(x[:, :n].astype(BF16).astype(F32), U32)
    hi = pltpu.bitcast(x[:, n:].astype(BF16).astype(F32), U32)
    return hi | (lo >> 16)


def _unpack_pairs(w):
    lo = pltpu.bitcast(w << 16, F32)
    hi = pltpu.bitcast(w & jnp.uint32(0xFFFF0000), F32)
    return jnp.concatenate([lo, hi], axis=1).astype(BF16)


def _window(off_s, j, NB, e):
    off = off_s[e * NB + j]
    lead = off & (SUBLANES - 1)
    return off - lead, lead


def _onehot_chunk(c, slot, pos, val, leads):
    tokens = pos.shape[1]
    pieces = []
    for p in range(MOE_BLOCK // P_CHUNK):
        row0 = c * MOE_BLOCK + p * P_CHUNK
        e, s0 = row0 // slot, row0 % slot
        s = (s0 + lax.broadcasted_iota(I32, (P_CHUNK, tokens), 0)).astype(F32)
        pieces.append(jnp.where(pos[e:e + 1, :] + leads[e] == s, val[e:e + 1, :], 0.0))
    return jnp.concatenate(pieces, axis=0).astype(BF16)


def _slot_copies(j, NB, off_s, slot, buf, lst, sem, to_list):
    copies = []
    for e in range(N_EXPERTS):
        start, _ = _window(off_s, j, NB, e)
        b = buf.at[e * slot:(e + 1) * slot]
        l = lst.at[e, pl.ds(pl.multiple_of(start, SUBLANES), slot)]
        copies.append(pltpu.make_async_copy(b, l, sem) if to_list else pltpu.make_async_copy(l, b, sem))
    return copies


def _leads(off_s, j, NB):
    return [_window(off_s, j, NB, e)[1].astype(F32) for e in range(N_EXPERTS)]


def _picks(gsel_ref, tri_ref):
    g = gsel_ref[...]
    sel = g >= 0.0
    self_ = jnp.where(sel, 1.0, 0.0)
    pos = jnp.dot(self_.astype(BF16), tri_ref[...], preferred_element_type=F32)
    return self_, jnp.where(sel, g, 0.0), pos


def _on_slot(maxn_s, jj, tokens, fn):
    def pick(sizes):
        if not sizes:
            return lambda: fn(tokens + SLOT_SIZES[-1])
        fits = maxn_s[jj] + (SUBLANES - 1) <= sizes[0]
        return lambda: lax.cond(fits, lambda: fn(sizes[0]), pick(sizes[1:]))
    pick([s for s in SLOT_SIZES if s < tokens])()


def _gather_kernel(cnt_s, off_s, maxn_s, h2_ref, gsel_ref, tri_ref, list_ref, out_buf, carry_s, zero_buf,
                   sems, *, NB, cap):
    j = pl.program_id(0)
    TB = h2_ref.shape[0]
    L = list_ref.shape[1]
    self_, _, pos = _picks(gsel_ref, tri_ref)
    buf = out_buf.at[j % 2]
    leads = _leads(off_s, j, NB)

    @pl.when(j == 0)
    def _():
        carry_s[...] = jnp.zeros(carry_s.shape, U32)

    def compute(slot):
        h2 = h2_ref[...]
        total = N_EXPERTS * slot
        for r0 in range(0, total, GATHER_ROWS):
            r1 = min(r0 + GATHER_ROWS, total)
            onehot = jnp.concatenate(
                [_onehot_chunk(c, slot, pos, self_, leads)
                 for c in range(r0 // MOE_BLOCK, r1 // MOE_BLOCK)], axis=0)
            rows = jnp.dot(onehot, h2, preferred_element_type=F32)
            buf[r0:r1, :] = _pack_pairs(rows)
        row = lax.broadcasted_iota(I32, (SUBLANES, PACKED_W), 0)
        for e in range(N_EXPERTS):
            _, lead = _window(off_s, j, NB, e)
            head = buf[e * slot:e * slot + SUBLANES, :]
            buf[e * slot:e * slot + SUBLANES, :] = jnp.where(row < lead, carry_s[e], head)
            filled = lead + cnt_s[e * NB + j]
            last = jnp.minimum((filled // SUBLANES) * SUBLANES, slot - SUBLANES)
            carry_s[e] = buf[pl.ds(pl.multiple_of(e * slot + last, SUBLANES), SUBLANES), :]

    def start_copies(jj):
        def fn(slot):
            for cp in _slot_copies(jj, NB, off_s, slot, out_buf.at[jj % 2], list_ref, sems.at[jj % 2], True):
                cp.start()
        _on_slot(maxn_s, jj, TB, fn)

    def wait_copies(jj):
        def fn(slot):
            for cp in _slot_copies(jj, NB, off_s, slot, out_buf.at[jj % 2], list_ref, sems.at[jj % 2], True):
                cp.wait()
        _on_slot(maxn_s, jj, TB, fn)

    _on_slot(maxn_s, j, TB, compute)

    @pl.when(j > 0)
    def _():
        wait_copies(jnp.maximum(j - 1, 0))

    start_copies(j)

    @pl.when(j == NB - 1)
    def _():
        wait_copies(j)
        zero_buf[...] = jnp.zeros(zero_buf.shape, U32)
        sem = sems.at[0]
        for r in range(-(-(L - cap) // ZERO_ROWS)):
            start = min(cap + r * ZERO_ROWS, L - ZERO_ROWS)
            copies = [pltpu.make_async_copy(zero_buf, list_ref.at[e, start:start + ZERO_ROWS], sem)
                      for e in range(N_EXPERTS)]
            for cp in copies:
                cp.start()
            for cp in copies:
                cp.wait()


def _gather(tables, h2, gsel, L, cap):
    T = h2.shape[0]
    TB = min(MOE_BLOCK, T)
    NB = T // TB
    grid_spec = pltpu.PrefetchScalarGridSpec(
        num_scalar_prefetch=3,
        grid=(NB,),
        in_specs=[
            pl.BlockSpec((TB, D_MODEL), lambda j, *_: (j, 0)),
            pl.BlockSpec((N_EXPERTS, TB), lambda j, *_: (0, j)),
            pl.BlockSpec((TB, TB), lambda j, *_: (0, 0)),
        ],
        out_specs=pl.BlockSpec(memory_space=pl.ANY),
        scratch_shapes=[
            pltpu.VMEM((2, N_EXPERTS * (TB + SLOT_SIZES[-1]), PACKED_W), U32),
            pltpu.VMEM((N_EXPERTS, SUBLANES, PACKED_W), U32),
            pltpu.VMEM((ZERO_ROWS, PACKED_W), U32),
            pltpu.SemaphoreType.DMA((2,)),
        ],
    )
    return pl.pallas_call(
        functools.partial(_gather_kernel, NB=NB, cap=cap),
        grid_spec=grid_spec,
        out_shape=jax.ShapeDtypeStruct((N_EXPERTS, L, PACKED_W), U32),
        compiler_params=_params(("arbitrary",)),
        name="gather",
    )(*tables, h2, gsel, _strict_upper(TB))


def _experts_kernel(xs_ref, wg_ref, wu_ref, wd_ref, ys_ref, wg_s, wu_s, wd_s, *, cap):
    c = pl.program_id(1)
    live = c * EXPERT_TILE < cap

    @pl.when(c == 0)
    def _():
        wg_s[...] = wg_ref[...].astype(BF16)
        wu_s[...] = wu_ref[...].astype(BF16)
        wd_s[...] = wd_ref[...].astype(BF16)

    @pl.when(live)
    def _():
        xs = _unpack_pairs(xs_ref[...])
        hg = jnp.dot(xs, wg_s[...], preferred_element_type=F32)
        hu = jnp.dot(xs, wu_s[...], preferred_element_type=F32)
        hid = (hg * _sigmoid(hg) * hu).astype(BF16)
        ys_ref[...] = _pack_pairs(jnp.dot(hid, wd_s[...], preferred_element_type=F32))

    @pl.when(jnp.logical_not(live))
    def _():
        ys_ref[...] = jnp.zeros(ys_ref.shape, U32)


def _experts(xs, wg, wu, wd, cap):
    E, L, W = xs.shape
    D = wg.shape[1]
    last = -(-cap // EXPERT_TILE) - 1
    wblk = lambda e, c: (e, 0, 0)
    return pl.pallas_call(
        functools.partial(_experts_kernel, cap=cap),
        grid=(E, L // EXPERT_TILE),
        in_specs=[
            pl.BlockSpec((None, EXPERT_TILE, W), lambda e, c: (e, jnp.minimum(c, last), 0)),
            pl.BlockSpec((None, D, D), wblk),
            pl.BlockSpec((None, D, D), wblk),
            pl.BlockSpec((None, D, D), wblk),
        ],
        out_specs=pl.BlockSpec((None, EXPERT_TILE, W), lambda e, c: (e, c, 0)),
        out_shape=jax.ShapeDtypeStruct((E, L, W), U32),
        scratch_shapes=[pltpu.VMEM((D, D), BF16), pltpu.VMEM((D, D), BF16), pltpu.VMEM((D, D), BF16)],
        compiler_params=_params(("arbitrary", "arbitrary")),
        name="experts",
    )(xs, wg, wu, wd)


def _combine_kernel(off_s, maxn_s, x1_ref, gsel_ref, tri_ref, ys_ref, y_ref, in_buf, sems, *, NB):
    j = pl.program_id(0)
    TB = x1_ref.shape[0]
    buf = in_buf.at[j % 2]

    def start_copies(jj):
        def fn(slot):
            for cp in _slot_copies(jj, NB, off_s, slot, in_buf.at[jj % 2], ys_ref, sems.at[jj % 2], False):
                cp.start()
        _on_slot(maxn_s, jj, TB, fn)

    def wait_copies(jj):
        def fn(slot):
            for cp in _slot_copies(jj, NB, off_s, slot, in_buf.at[jj % 2], ys_ref, sems.at[jj % 2], False):
                cp.wait()
        _on_slot(maxn_s, jj, TB, fn)

    @pl.when(j == 0)
    def _():
        start_copies(j)

    @pl.when(j + 1 < NB)
    def _():
        start_copies(jnp.minimum(j + 1, NB - 1))

    _, gate, pos = _picks(gsel_ref, tri_ref)
    leads = _leads(off_s, j, NB)
    wait_copies(j)

    def compute(slot):
        acc = x1_ref[...]
        for c in range(N_EXPERTS * slot // MOE_BLOCK):
            pc = _onehot_chunk(c, slot, pos, gate, leads)
            yc = _unpack_pairs(buf[c * MOE_BLOCK:(c + 1) * MOE_BLOCK, :])
            acc = acc + lax.dot_general(pc, yc, (((0,), (0,)), ((), ())), preferred_element_type=F32)
        y_ref[...] = acc

    _on_slot(maxn_s, j, TB, compute)


def _combine(tables, x1, gsel, ys):
    _, off, maxn = tables
    T = x1.shape[0]
    TB = min(MOE_BLOCK, T)
    NB = T // TB
    grid_spec = pltpu.PrefetchScalarGridSpec(
        num_scalar_prefetch=2,
        grid=(NB,),
        in_specs=[
            pl.BlockSpec((TB, D_MODEL), lambda j, *_: (j, 0)),
            pl.BlockSpec((N_EXPERTS, TB), lambda j, *_: (0, j)),
            pl.BlockSpec((TB, TB), lambda j, *_: (0, 0)),
            pl.BlockSpec(memory_space=pl.ANY),
        ],
        out_specs=pl.BlockSpec((TB, D_MODEL), lambda j, *_: (j, 0)),
        scratch_shapes=[
            pltpu.VMEM((2, N_EXPERTS * (TB + SLOT_SIZES[-1]), PACKED_W), U32),
            pltpu.SemaphoreType.DMA((2,)),
        ],
    )
    return pl.pallas_call(
        functools.partial(_combine_kernel, NB=NB),
        grid_spec=grid_spec,
        out_shape=jax.ShapeDtypeStruct((T, D_MODEL), F32),
        compiler_params=_params(("arbitrary",)),
        name="combine",
    )(off, maxn, x1, gsel, _strict_upper(TB), ys)


def _rope_tables(seq):
    pos = jnp.arange(seq, dtype=I32)
    row = (pos // GRID_W).astype(F32)
    col = (pos % GRID_W).astype(F32)
    axis_dim = HEAD_DIM // 2
    inv_freq = ROPE_THETA ** (-jnp.arange(0, axis_dim, 2, dtype=F32) / axis_dim)
    ra = row[:, None] * inv_freq[None, :]
    ca = col[:, None] * inv_freq[None, :]
    cos = jnp.concatenate([jnp.cos(ra), jnp.cos(ra), jnp.cos(ca), jnp.cos(ca)], axis=-1)
    sin = jnp.concatenate([-jnp.sin(ra), jnp.sin(ra), -jnp.sin(ca), jnp.sin(ca)], axis=-1)
    reps = LANES // HEAD_DIM
    return jnp.tile(cos, (1, reps)), jnp.tile(sin, (1, reps))


def _head_mean_matrix(width):
    r = lax.broadcasted_iota(I32, (width, width), 0) // HEAD_DIM
    c = lax.broadcasted_iota(I32, (width, width), 1) // HEAD_DIM
    return jnp.where(r == c, 1.0 / HEAD_DIM, 0.0).astype(BF16)


def _block_diag(w):
    n, d, _ = w.shape
    eye = jnp.eye(n, dtype=w.dtype)
    return jnp.einsum('nde,nm->ndme', w, eye).reshape(n * d, n * d)


def _group_gate_weights(w_r, w_i):
    per = LANES // LRU_BLOCK_DIM
    groups = [jnp.concatenate([_block_diag(w_r[g * per:(g + 1) * per]),
                               _block_diag(w_i[g * per:(g + 1) * per])], axis=1)
              for g in range(LRU_GROUPS)]
    return jnp.stack(groups).astype(BF16)


def _layer(x, weights):
    (g1, w_in, qg, kg, cw, cb, wf, wb, bias, lam, wa, wl, g2, rwt, wg, wu, wd) = weights
    B, S, D = x.shape
    T = B * S
    x2d = x.reshape(T, D)
    cos, sin = _rope_tables(S)
    hq = _head_mean_matrix(ATTN_W)
    hk = _head_mean_matrix(KV_W)
    q, ka, kb, v, xr, ga = _proj(x2d, S, g1, w_in, qg, kg, cos, sin, hq, hk)
    lru = _lru(xr.reshape(B, S, LRU_W), ga.reshape(B, S, LRU_W), cw, cb, wf, wb, bias, lam)
    att = _attn(q.reshape(B, S, ATTN_W), ka.reshape(B, S, KV_W), kb.reshape(B, S, KV_W),
                v.reshape(B, S, KV_W))
    x1, h2, aff_t = _outproj(att.reshape(T, ATTN_W), lru.reshape(T, LRU_W), x2d, wa, wl, g2, rwt)

    cap = max(1, EXPERT_CAPACITY_FACTOR * T // N_EXPERTS)
    TB = min(MOE_BLOCK, T)
    NB = T // TB
    L = -(-(cap + TB + SLOT_SIZES[-1]) // EXPERT_TILE) * EXPERT_TILE
    gsel, cnt, off, maxn = _select(aff_t, cap, NB)
    tables = (cnt.reshape(-1), off.reshape(-1), maxn[0])
    xs = _gather(tables, h2, gsel, L, cap)
    ys = _experts(xs, wg, wu, wd, cap)
    y = _combine(tables, x1, gsel, ys)
    return y.reshape(B, S, D)


def kernel(x_prompt, x_sample, norm1_g, w_in, q_norm_g, k_norm_g, conv_w, conv_b, lru_w_r, lru_b_r,
           lru_w_i, lru_b_i, lru_lambda, w_out, norm2_g, router_w, exp_w_gate, exp_w_up, exp_w_down):
    layers = []
    for l in range(norm1_g.shape[0]):
        bias = jnp.concatenate([lru_b_r[l, 0], lru_b_i[l, 0], lru_b_r[l, 1], lru_b_i[l, 1]])[None, :]
        layers.append((
            norm1_g[l][None, :], w_in[l].astype(BF16),
            jnp.tile(q_norm_g[l], N_Q_HEADS)[None, :], jnp.tile(k_norm_g[l], N_KV_HEADS)[None, :],
            conv_w[l], conv_b[l][None, :],
            _group_gate_weights(lru_w_r[l, 0], lru_w_i[l, 0]), _group_gate_weights(lru_w_r[l, 1], lru_w_i[l, 1]),
            bias, lru_lambda[l],
            w_out[l, :ATTN_W].astype(BF16), w_out[l, ATTN_W:].astype(BF16),
            norm2_g[l][None, :], router_w[l].T.astype(BF16),
            exp_w_gate[l], exp_w_up[l], exp_w_down[l],
        ))
    ys = []
    for x in (x_prompt, x_sample):
        for weights in layers:
            x = _layer(x, weights)
        ys.append(x)
    return tuple(ys)
```

```python
import functools

import jax
import jax.numpy as jnp
from jax import lax
from jax.experimental import pallas as pl
from jax.experimental.pallas import tpu as pltpu

F32 = jnp.float32
BF16 = jnp.bfloat16
I32 = jnp.int32
U32 = jnp.uint32

D_MODEL = 1024
GRID_W = 64
HEAD_DIM = 64
N_Q_HEADS = 8
N_KV_HEADS = 2
ATTN_W = N_Q_HEADS * HEAD_DIM
KV_W = N_KV_HEADS * HEAD_DIM
LRU_W = D_MODEL - ATTN_W
LRU_BLOCKS = 8
LRU_BLOCK_DIM = LRU_W // LRU_BLOCKS
CONV_W = 4
LRU_C = 8.0
N_EXPERTS = 16
EXPERT_CAPACITY_FACTOR = 2
ROPE_THETA = 10000.0
EPS = 1e-6
IN_W = ATTN_W + 2 * KV_W + 2 * LRU_W
LOG2E = 1.4426950408889634

LANES = 128
SUBLANES = 8
VMEM_LIMIT = 56 * 1024 * 1024

TOKEN_TILE = 1024
Q_TILE = 512
LRU_CHUNK = 512
LRU_GROUPS = LRU_W // LANES
MOE_BLOCK = 256
GATHER_ROWS = 1024
EXPERT_TILE = 1024
ZERO_ROWS = 256
SLOT_SIZES = (48, 64)
P_CHUNK = 16
PACKED_W = D_MODEL // 2


def _params(sem, vmem=VMEM_LIMIT):
    return pltpu.CompilerParams(dimension_semantics=sem, vmem_limit_bytes=vmem)


def _swap16(x, lane):
    fwd = pltpu.roll(x, 16, axis=1)
    bwd = pltpu.roll(x, LANES - 16, axis=1)
    return jnp.where((lane & 16) == 0, bwd, fwd)


def _split_dot(x, w):
    hi = x.astype(BF16)
    lo = (x - hi.astype(F32)).astype(BF16)
    return (jnp.dot(hi, w, preferred_element_type=F32)
            + jnp.dot(lo, w, preferred_element_type=F32))


def _proj_kernel(x_ref, g1_ref, w_ref, qg_ref, kg_ref, cos_ref, sin_ref, hq_ref, hk_ref,
                 q_ref, ka_ref, kb_ref, v_ref, xr_ref, ga_ref):
    x = x_ref[...]
    ms = jnp.mean(x * x, axis=-1, keepdims=True)
    h = (x * lax.rsqrt(ms + EPS) * g1_ref[...]).astype(BF16)
    proj = jnp.dot(h, w_ref[...], preferred_element_type=F32)
    q = proj[:, :ATTN_W]
    k = proj[:, ATTN_W:ATTN_W + KV_W]
    v = proj[:, ATTN_W + KV_W:ATTN_W + 2 * KV_W]
    xr = proj[:, ATTN_W + 2 * KV_W:ATTN_W + 2 * KV_W + LRU_W]
    gate = proj[:, ATTN_W + 2 * KV_W + LRU_W:]

    cos = cos_ref[...]
    sin = sin_ref[...]
    lane = lax.broadcasted_iota(I32, cos.shape, 1)

    qn = q * lax.rsqrt(_split_dot(q * q, hq_ref[...]) + EPS) * qg_ref[...]
    kn = k * lax.rsqrt(_split_dot(k * k, hk_ref[...]) + EPS) * kg_ref[...]

    qscale = (HEAD_DIM ** -0.5) * LOG2E
    for p in range(ATTN_W // LANES):
        qc = qn[:, p * LANES:(p + 1) * LANES]
        qr = qc * cos + _swap16(qc, lane) * sin
        q_ref[:, p * LANES:(p + 1) * LANES] = (qr * qscale).astype(BF16)
    kr = kn * cos + _swap16(kn, lane) * sin
    ka_ref[...] = kr.astype(BF16)
    kb_ref[...] = pltpu.roll(kr, HEAD_DIM, axis=1).astype(BF16)
    v_ref[...] = v.astype(BF16)
    xr_ref[...] = xr
    ga_ref[...] = 0.5 * gate * (1.0 + jnp.tanh(0.7978845608028654 * (gate + 0.044715 * gate * gate * gate)))


def _proj(x2d, seq, g1, w_in, qg, kg, cos, sin, hq, hk):
    T = x2d.shape[0]
    tm = min(TOKEN_TILE, seq)
    nseq = seq // tm
    row = lambda i: (i, 0)
    const = lambda i: (0, 0)
    pos = lambda i: (i % nseq, 0)
    out_shapes = (
        jax.ShapeDtypeStruct((T, ATTN_W), BF16),
        jax.ShapeDtypeStruct((T, KV_W), BF16),
        jax.ShapeDtypeStruct((T, KV_W), BF16),
        jax.ShapeDtypeStruct((T, KV_W), BF16),
        jax.ShapeDtypeStruct((T, LRU_W), F32),
        jax.ShapeDtypeStruct((T, LRU_W), F32),
    )
    return pl.pallas_call(
        _proj_kernel,
        grid=(T // tm,),
        in_specs=[
            pl.BlockSpec((tm, D_MODEL), row),
            pl.BlockSpec((1, D_MODEL), const),
            pl.BlockSpec((D_MODEL, IN_W), const),
            pl.BlockSpec((1, ATTN_W), const),
            pl.BlockSpec((1, KV_W), const),
            pl.BlockSpec((tm, LANES), pos),
            pl.BlockSpec((tm, LANES), pos),
            pl.BlockSpec((ATTN_W, ATTN_W), const),
            pl.BlockSpec((KV_W, KV_W), const),
        ],
        out_specs=(
            pl.BlockSpec((tm, ATTN_W), row),
            pl.BlockSpec((tm, KV_W), row),
            pl.BlockSpec((tm, KV_W), row),
            pl.BlockSpec((tm, KV_W), row),
            pl.BlockSpec((tm, LRU_W), row),
            pl.BlockSpec((tm, LRU_W), row),
        ),
        out_shape=out_shapes,
        compiler_params=_params(("parallel",)),
        name="proj",
    )(x2d, g1, w_in, qg, kg, cos, sin, hq, hk)


def _sigmoid(x):
    return 0.5 * jnp.tanh(0.5 * x) + 0.5


def _scan8(a, b, h_in, first_row, shifts):
    b = b + jnp.where(first_row, a * h_in, 0.0)
    a = jnp.where(first_row, 0.0, a)
    for d in shifts:
        b = b + a * pltpu.roll(b, d, axis=0)
        a = a * pltpu.roll(a, d, axis=0)
    return b


def _lru_kernel(xr_ref, ga_ref, cw_ref, cb_ref, wf_ref, wb_ref, bias_ref, lam_ref, o_ref,
                xpad_s, xc_s, hf_s, hb_s, pre_f, pre_b):
    S = xr_ref.shape[0]
    C = LRU_W
    CH = min(LRU_CHUNK, S)
    NC = S // CH
    G = CH // SUBLANES
    left = CONV_W // 2

    xpad_s[0:SUBLANES, :] = jnp.zeros((SUBLANES, C), F32)
    xpad_s[SUBLANES + S:2 * SUBLANES + S, :] = jnp.zeros((SUBLANES, C), F32)
    xpad_s[SUBLANES:SUBLANES + S, :] = xr_ref[...]
    cw_half = 0.5 * cw_ref[...]
    cb_half = 0.5 * cb_ref[...]
    for c in range(NC):
        u = xpad_s[c * CH:c * CH + CH + 2 * SUBLANES, :]
        acc = cb_half + jnp.zeros((CH, C), F32)
        for tap in range(CONV_W):
            shift = (left - tap) % (CH + 2 * SUBLANES)
            ush = u if shift == 0 else pltpu.roll(u, shift, axis=0)
            acc = acc + ush[SUBLANES:SUBLANES + CH, :] * cw_half[tap:tap + 1, :]
        xc_s[c * CH:(c + 1) * CH, :] = acc

    z = -lam_ref[...]
    c1 = (-0.5 * LRU_C * LOG2E) * (jnp.maximum(z, 0.0) + jnp.log1p(jnp.exp(-jnp.abs(z))))
    c1_f = c1[0:1, :]
    c1_b = c1[1:2, :]
    bias_half = 0.5 * bias_ref[...]
    row = lax.broadcasted_iota(I32, (SUBLANES, C), 0)
    is_first = row == 0
    is_last = row == SUBLANES - 1

    def gate_preacts(x_bf, w_ref, b0, pre_ref):
        for g in range(LRU_GROUPS):
            res = jnp.dot(x_bf[:, g * LANES:(g + 1) * LANES], w_ref[g], preferred_element_type=F32)
            pre_ref[:, g * LANES:(g + 1) * LANES] = (
                res[:, :LANES] + bias_half[:, b0 + g * LANES:b0 + (g + 1) * LANES])
            pre_ref[:, C + g * LANES:C + (g + 1) * LANES] = (
                res[:, LANES:] + bias_half[:, b0 + C + g * LANES:b0 + C + (g + 1) * LANES])

    def gates(pre_ref, g, x8_half, c1_d):
        tr = jnp.tanh(pre_ref[pl.ds(g * SUBLANES, SUBLANES), 0:C])
        ti = jnp.tanh(pre_ref[pl.ds(g * SUBLANES, SUBLANES), C:2 * C])
        a = jnp.exp2(c1_d * tr + c1_d)
        return a, jnp.sqrt(1.0 - a * a) * ((ti + 1.0) * x8_half)

    hf = jnp.zeros((SUBLANES, C), F32)
    hb = jnp.zeros((SUBLANES, C), F32)
    for c in range(NC):
        cf = c
        cb = NC - 1 - c
        gate_preacts(xc_s[cf * CH:(cf + 1) * CH, :].astype(BF16), wf_ref, 0, pre_f)
        gate_preacts(xc_s[cb * CH:(cb + 1) * CH, :].astype(BF16), wb_ref, 2 * C, pre_b)

        def body(g, carry, cf=cf, cb=cb):
            hf, hb = carry
            rf = pl.multiple_of(cf * CH + g * SUBLANES, SUBLANES)
            a, b = gates(pre_f, g, xc_s[pl.ds(rf, SUBLANES), :], c1_f)
            h = _scan8(a, b, hf, is_first, (1, 2, 4))
            hf_s[pl.ds(rf, SUBLANES), :] = h
            hf = jnp.broadcast_to(h[SUBLANES - 1:SUBLANES, :], (SUBLANES, C))
            gb = G - 1 - g
            rb = pl.multiple_of(cb * CH + gb * SUBLANES, SUBLANES)
            a, b = gates(pre_b, gb, xc_s[pl.ds(rb, SUBLANES), :], c1_b)
            h = _scan8(a, b, hb, is_last, (SUBLANES - 1, SUBLANES - 2, SUBLANES - 4))
            hb_s[pl.ds(rb, SUBLANES), :] = h
            hb = jnp.broadcast_to(h[0:1, :], (SUBLANES, C))
            return hf, hb

        hf, hb = lax.fori_loop(0, G, body, (hf, hb), unroll=2)

    o_ref[...] = ((hf_s[...] + hb_s[...]) * ga_ref[...]).astype(BF16)


def _lru(xr, ga, cw, cb, wf, wb, bias, lam):
    B, S, C = xr.shape
    CH = min(LRU_CHUNK, S)
    blk = lambda b: (b, 0, 0)
    const2 = lambda b: (0, 0)
    const3 = lambda b: (0, 0, 0)
    return pl.pallas_call(
        _lru_kernel,
        grid=(B,),
        in_specs=[
            pl.BlockSpec((None, S, C), blk),
            pl.BlockSpec((None, S, C), blk),
            pl.BlockSpec((CONV_W, C), const2),
            pl.BlockSpec((1, C), const2),
            pl.BlockSpec((LRU_GROUPS, LANES, 2 * LANES), const3),
            pl.BlockSpec((LRU_GROUPS, LANES, 2 * LANES), const3),
            pl.BlockSpec((1, 4 * C), const2),
            pl.BlockSpec((2, C), const2),
        ],
        out_specs=pl.BlockSpec((None, S, C), blk),
        out_shape=jax.ShapeDtypeStruct((B, S, C), BF16),
        scratch_shapes=[
            pltpu.VMEM((S + 2 * SUBLANES, C), F32),
            pltpu.VMEM((S, C), F32),
            pltpu.VMEM((S, C), F32),
            pltpu.VMEM((S, C), F32),
            pltpu.VMEM((CH, 2 * C), F32),
            pltpu.VMEM((CH, 2 * C), F32),
        ],
        compiler_params=_params(("parallel",)),
        name="lru",
    )(xr, ga, cw, cb, wf, wb, bias, lam)


ONES_ROWS = 16


def _attn_kernel(q_ref, ka_ref, kb_ref, v_ref, o_ref, vt_s):
    tq = q_ref.shape[0]
    S = ka_ref.shape[0]

    @pl.when(pl.program_id(1) == 0)
    def _():
        vt = v_ref[...].astype(F32).T.astype(BF16)
        for kv in range(N_KV_HEADS):
            vt_s[kv, 0:HEAD_DIM, :] = vt[kv * HEAD_DIM:(kv + 1) * HEAD_DIM, :]
            vt_s[kv, HEAD_DIM:HEAD_DIM + ONES_ROWS, :] = jnp.ones((ONES_ROWS, S), BF16)

    lane = lax.broadcasted_iota(I32, (tq, LANES), 1)
    low = lane < HEAD_DIM
    contract_last = (((1,), (1,)), ((), ()))
    q_per_kv = N_Q_HEADS // N_KV_HEADS
    heads = [(p, half) for p in range(ATTN_W // LANES) for half in range(2)]

    def scores(p, half):
        qp = q_ref[:, p * LANES:(p + 1) * LANES]
        qm = jnp.where(low if half == 0 else jnp.logical_not(low), qp, jnp.zeros_like(qp))
        k_ref = ka_ref if (2 * p) // q_per_kv == half else kb_ref
        st = lax.dot_general(k_ref[...], qm, contract_last, preferred_element_type=F32)
        return st.astype(BF16)

    halves = []
    ahead = [scores(*heads[0]), scores(*heads[1])]
    for i, (p, half) in enumerate(heads):
        sb = ahead.pop(0)
        if i + 2 < len(heads):
            ahead.append(scores(*heads[i + 2]))
        kv = (2 * p) // q_per_kv
        pt = jnp.exp2(sb - jnp.max(sb, axis=0, keepdims=True))
        ot = jnp.dot(vt_s[kv], pt, preferred_element_type=F32)
        halves.append(ot[:HEAD_DIM, :] / ot[HEAD_DIM:HEAD_DIM + 1, :])
        if half == 1:
            pair_t = jnp.concatenate(halves, axis=0)
            o_ref[:, p * LANES:(p + 1) * LANES] = pair_t.T.astype(BF16)
            halves = []


def _attn(q, ka, kb, v):
    B, S, _ = q.shape
    tq = min(Q_TILE, S)
    qblk = lambda b, i: (b, i, 0)
    kblk = lambda b, i: (b, 0, 0)
    return pl.pallas_call(
        _attn_kernel,
        grid=(B, S // tq),
        in_specs=[
            pl.BlockSpec((None, tq, ATTN_W), qblk),
            pl.BlockSpec((None, S, KV_W), kblk),
            pl.BlockSpec((None, S, KV_W), kblk),
            pl.BlockSpec((None, S, KV_W), kblk),
        ],
        out_specs=pl.BlockSpec((None, tq, ATTN_W), qblk),
        out_shape=jax.ShapeDtypeStruct((B, S, ATTN_W), BF16),
        scratch_shapes=[pltpu.VMEM((N_KV_HEADS, HEAD_DIM + ONES_ROWS, S), BF16)],
        compiler_params=_params(("parallel", "arbitrary")),
        name="attn",
    )(q, ka, kb, v)


def _outproj_kernel(a_ref, l_ref, x_ref, wa_ref, wl_ref, g2_ref, rw_ref, x1_ref, h2_ref, aff_ref):
    x1 = (x_ref[...]
          + jnp.dot(a_ref[...], wa_ref[...], preferred_element_type=F32)
          + jnp.dot(l_ref[...], wl_ref[...], preferred_element_type=F32))
    x1_ref[...] = x1
    ms = jnp.mean(x1 * x1, axis=-1, keepdims=True)
    h2 = (x1 * lax.rsqrt(ms + EPS) * g2_ref[...]).astype(BF16)
    h2_ref[...] = h2
    logits = lax.dot_general(rw_ref[...], h2, (((1,), (1,)), ((), ())), preferred_element_type=F32)
    m = jnp.max(logits, axis=0, keepdims=True)
    e = jnp.exp(logits - m)
    aff_ref[...] = e / jnp.sum(e, axis=0, keepdims=True)


def _outproj(attn2d, lru2d, x2d, wa, wl, g2, rwt):
    T = x2d.shape[0]
    tm = min(TOKEN_TILE, T)
    row = lambda i: (i, 0)
    const = lambda i: (0, 0)
    return pl.pallas_call(
        _outproj_kernel,
        grid=(T // tm,),
        in_specs=[
            pl.BlockSpec((tm, ATTN_W), row),
            pl.BlockSpec((tm, LRU_W), row),
            pl.BlockSpec((tm, D_MODEL), row),
            pl.BlockSpec((ATTN_W, D_MODEL), const),
            pl.BlockSpec((LRU_W, D_MODEL), const),
            pl.BlockSpec((1, D_MODEL), const),
            pl.BlockSpec((N_EXPERTS, D_MODEL), const),
        ],
        out_specs=(
            pl.BlockSpec((tm, D_MODEL), row),
            pl.BlockSpec((tm, D_MODEL), row),
            pl.BlockSpec((N_EXPERTS, tm), lambda i: (0, i)),
        ),
        out_shape=(
            jax.ShapeDtypeStruct((T, D_MODEL), F32),
            jax.ShapeDtypeStruct((T, D_MODEL), BF16),
            jax.ShapeDtypeStruct((N_EXPERTS, T), F32),
        ),
        compiler_params=_params(("parallel",)),
        name="outproj",
    )(attn2d, lru2d, x2d, wa, wl, g2, rwt)


def _select_kernel(aff_ref, tri_ref, trinb_ref, gsel_ref, cnt_ref, off_ref, maxn_ref, *, cap, T, NB):
    TB = T // NB
    CW = min(2048, T)
    E = N_EXPERTS

    def count_where(pred):
        def body(i, acc):
            bits = pltpu.bitcast(aff_ref[:, pl.ds(pl.multiple_of(i * CW, LANES), CW)], I32)
            m = pred(bits).astype(I32)
            part = m[:, 0:LANES]
            for qd in range(1, CW // LANES):
                part = part + m[:, qd * LANES:(qd + 1) * LANES]
            return acc + part
        acc = lax.fori_loop(0, T // CW, body, jnp.zeros((E, LANES), I32))
        return jnp.sum(acc, axis=1, keepdims=True)

    def bit_step(it, thr):
        cand = thr | jnp.left_shift(jnp.int32(1), 30 - it)
        n = count_where(lambda bits: bits >= cand)
        return jnp.where(n >= cap, cand, thr)

    thr = lax.fori_loop(0, 31, bit_step, jnp.zeros((E, 1), I32))
    n_gt = count_where(lambda bits: bits > thr)
    need = (cap - n_gt).astype(F32)

    lane_nb = lax.broadcasted_iota(I32, (E, NB), 1)

    def blk_step(j, carry):
        eq_before, cnt_acc = carry
        aff = aff_ref[:, pl.ds(pl.multiple_of(j * TB, LANES), TB)]
        bits = pltpu.bitcast(aff, I32)
        gt = bits > thr
        eq = bits == thr
        eqf = jnp.where(eq, 1.0, 0.0)
        pref = jnp.dot(eqf.astype(BF16), tri_ref[...], preferred_element_type=F32)
        take = jnp.logical_and(eq, (eq_before + pref) < need)
        sel = jnp.logical_or(gt, take)
        gsel_ref[:, pl.ds(pl.multiple_of(j * TB, LANES), TB)] = jnp.where(sel, aff, -1.0)
        cnt = jnp.sum(jnp.where(sel, 1.0, 0.0), axis=1, keepdims=True)
        cnt_acc = jnp.where(lane_nb == j, cnt, cnt_acc)
        return eq_before + jnp.sum(eqf, axis=1, keepdims=True), cnt_acc

    _, cnt = lax.fori_loop(0, NB, blk_step, (jnp.zeros((E, 1), F32), jnp.zeros((E, NB), F32)))
    off = jnp.dot(cnt.astype(BF16), trinb_ref[...], preferred_element_type=F32)
    cnt_ref[...] = cnt.astype(I32)
    off_ref[...] = off.astype(I32)
    maxn_ref[...] = jnp.broadcast_to(jnp.max(cnt, axis=0, keepdims=True), (SUBLANES, NB)).astype(I32)


def _strict_upper(n):
    r = lax.broadcasted_iota(I32, (n, n), 0)
    c = lax.broadcasted_iota(I32, (n, n), 1)
    return (r < c).astype(BF16)


def _select(aff_t, cap, NB):
    E, T = aff_t.shape
    TB = T // NB
    full = lambda shape: pl.BlockSpec(shape, lambda i: (0,) * len(shape))
    return pl.pallas_call(
        functools.partial(_select_kernel, cap=cap, T=T, NB=NB),
        grid=(1,),
        in_specs=[full((E, T)), full((TB, TB)), full((NB, NB))],
        out_specs=(full((E, T)), full((E, NB)), full((E, NB)), full((SUBLANES, NB))),
        out_shape=(
            jax.ShapeDtypeStruct((E, T), F32),
            jax.ShapeDtypeStruct((E, NB), I32),
            jax.ShapeDtypeStruct((E, NB), I32),
            jax.ShapeDtypeStruct((SUBLANES, NB), I32),
        ),
        compiler_params=_params(("arbitrary",)),
        name="select",
    )(aff_t, _strict_upper(TB), _strict_upper(NB))


def _pack_pairs(x):
    n = x.shape[1] // 2
    lo = pltpu.bitcast(x[:, :n].astype(BF16).astype(F32), U32)
    hi = pltpu.bitcast(x[:, n:].astype(BF16).astype(F32), U32)
    return hi | (lo >> 16)


def _unpack_pairs(w):
    lo = pltpu.bitcast(w << 16, F32)
    hi = pltpu.bitcast(w & jnp.uint32(0xFFFF0000), F32)
    return jnp.concatenate([lo, hi], axis=1).astype(BF16)


def _window(off_s, j, NB, e):
    off = off_s[e * NB + j]
    lead = off & (SUBLANES - 1)
    return off - lead, lead


def _onehot_chunk(c, slot, pos, val, leads):
    tokens = pos.shape[1]
    pieces = []
    for p in range(MOE_BLOCK // P_CHUNK):
        row0 = c * MOE_BLOCK + p * P_CHUNK
        e, s0 = row0 // slot, row0 % slot
        s = (s0 + lax.broadcasted_iota(I32, (P_CHUNK, tokens), 0)).astype(F32)
        pieces.append(jnp.where(pos[e:e + 1, :] + leads[e] == s, val[e:e + 1, :], 0.0))
    return jnp.concatenate(pieces, axis=0).astype(BF16)


def _slot_copies(j, NB, off_s, slot, buf, lst, sem, to_list):
    copies = []
    for e in range(N_EXPERTS):
        start, _ = _window(off_s, j, NB, e)
        b = buf.at[e * slot:(e + 1) * slot]
        l = lst.at[e, pl.ds(pl.multiple_of(start, SUBLANES), slot)]
        copies.append(pltpu.make_async_copy(b, l, sem) if to_list else pltpu.make_async_copy(l, b, sem))
    return copies


def _leads(off_s, j, NB):
    return [_window(off_s, j, NB, e)[1].astype(F32) for e in range(N_EXPERTS)]


def _blocks_per_step(NB):
    return 2 if NB % 2 == 0 else 1


def _picks(g, tri_ref):
    sel = g >= 0.0
    self_ = jnp.where(sel, 1.0, 0.0)
    pos = jnp.dot(self_.astype(BF16), tri_ref[...], preferred_element_type=F32)
    return self_, jnp.where(sel, g, 0.0), pos


def _on_slot(maxn_s, jj, tokens, fn):
    def pick(sizes):
        if not sizes:
            return lambda: fn(tokens + SLOT_SIZES[-1])
        fits = maxn_s[jj] + (SUBLANES - 1) <= sizes[0]
        return lambda: lax.cond(fits, lambda: fn(sizes[0]), pick(sizes[1:]))
    pick([s for s in SLOT_SIZES if s < tokens])()


def _gather_kernel(cnt_s, off_s, maxn_s, h2_ref, gsel_ref, tri_ref, list_ref, out_buf, carry_s, zero_buf,
                   sems, *, NB, TB, cap):
    L = list_ref.shape[1]

    def compute(j, rows, slot):
        buf = out_buf.at[j % 2]
        self_, _, pos = _picks(gsel_ref[:, rows], tri_ref)
        leads = _leads(off_s, j, NB)
        h2 = h2_ref[rows, :]
        total = N_EXPERTS * slot
        for r0 in range(0, total, GATHER_ROWS):
            r1 = min(r0 + GATHER_ROWS, total)
            onehot = jnp.concatenate(
                [_onehot_chunk(c, slot, pos, self_, leads)
                 for c in range(r0 // MOE_BLOCK, r1 // MOE_BLOCK)], axis=0)
            picked = jnp.dot(onehot, h2, preferred_element_type=F32)
            buf[r0:r1, :] = _pack_pairs(picked)
        row = lax.broadcasted_iota(I32, (SUBLANES, PACKED_W), 0)
        for e in range(N_EXPERTS):
            _, lead = _window(off_s, j, NB, e)
            head = buf[e * slot:e * slot + SUBLANES, :]
            buf[e * slot:e * slot + SUBLANES, :] = jnp.where(row < lead, carry_s[e], head)
            filled = lead + cnt_s[e * NB + j]
            last = jnp.minimum((filled // SUBLANES) * SUBLANES, slot - SUBLANES)
            carry_s[e] = buf[pl.ds(pl.multiple_of(e * slot + last, SUBLANES), SUBLANES), :]

    def start_copies(jj):
        def fn(slot):
            for cp in _slot_copies(jj, NB, off_s, slot, out_buf.at[jj % 2], list_ref, sems.at[jj % 2], True):
                cp.start()
        _on_slot(maxn_s, jj, TB, fn)

    def wait_copies(jj):
        def fn(slot):
            for cp in _slot_copies(jj, NB, off_s, slot, out_buf.at[jj % 2], list_ref, sems.at[jj % 2], True):
                cp.wait()
        _on_slot(maxn_s, jj, TB, fn)

    def block(j, rows):
        @pl.when(j == 0)
        def _():
            carry_s[...] = jnp.zeros(carry_s.shape, U32)

        _on_slot(maxn_s, j, TB, functools.partial(compute, j, rows))

        @pl.when(j > 0)
        def _():
            wait_copies(jnp.maximum(j - 1, 0))

        start_copies(j)

        @pl.when(j == NB - 1)
        def _():
            wait_copies(j)
            zero_buf[...] = jnp.zeros(zero_buf.shape, U32)
            sem = sems.at[0]
            for r in range(-(-(L - cap) // ZERO_ROWS)):
                start = min(cap + r * ZERO_ROWS, L - ZERO_ROWS)
                copies = [pltpu.make_async_copy(zero_buf, list_ref.at[e, start:start + ZERO_ROWS], sem)
                          for e in range(N_EXPERTS)]
                for cp in copies:
                    cp.start()
                for cp in copies:
                    cp.wait()

    per_step = h2_ref.shape[0] // TB
    for s in range(per_step):
        block(pl.program_id(0) * per_step + s, slice(s * TB, (s + 1) * TB))


def _gather(tables, h2, gsel, L, cap):
    T = h2.shape[0]
    TB = min(MOE_BLOCK, T)
    NB = T // TB
    rows = _blocks_per_step(NB) * TB
    grid_spec = pltpu.PrefetchScalarGridSpec(
        num_scalar_prefetch=3,
        grid=(T // rows,),
        in_specs=[
            pl.BlockSpec((rows, D_MODEL), lambda j, *_: (j, 0)),
            pl.BlockSpec((N_EXPERTS, rows), lambda j, *_: (0, j)),
            pl.BlockSpec((TB, TB), lambda j, *_: (0, 0)),
        ],
        out_specs=pl.BlockSpec(memory_space=pl.ANY),
        scratch_shapes=[
            pltpu.VMEM((2, N_EXPERTS * (TB + SLOT_SIZES[-1]), PACKED_W), U32),
            pltpu.VMEM((N_EXPERTS, SUBLANES, PACKED_W), U32),
            pltpu.VMEM((ZERO_ROWS, PACKED_W), U32),
            pltpu.SemaphoreType.DMA((2,)),
        ],
    )
    return pl.pallas_call(
        functools.partial(_gather_kernel, NB=NB, TB=TB, cap=cap),
        grid_spec=grid_spec,
        out_shape=jax.ShapeDtypeStruct((N_EXPERTS, L, PACKED_W), U32),
        compiler_params=_params(("arbitrary",)),
        name="gather",
    )(*tables, h2, gsel, _strict_upper(TB))


def _experts_kernel(xs_ref, wg_ref, wu_ref, wd_ref, ys_ref, wg_s, wu_s, wd_s, *, cap):
    c = pl.program_id(1)
    live = c * EXPERT_TILE < cap

    @pl.when(c == 0)
    def _():
        wg_s[...] = wg_ref[...].astype(BF16)
        wu_s[...] = wu_ref[...].astype(BF16)
        wd_s[...] = wd_ref[...].astype(BF16)

    @pl.when(live)
    def _():
        xs = _unpack_pairs(xs_ref[...])
        hg = jnp.dot(xs, wg_s[...], preferred_element_type=F32)
        hu = jnp.dot(xs, wu_s[...], preferred_element_type=F32)
        hid = (hg * _sigmoid(hg) * hu).astype(BF16)
        ys_ref[...] = _pack_pairs(jnp.dot(hid, wd_s[...], preferred_element_type=F32))

    @pl.when(jnp.logical_not(live))
    def _():
        ys_ref[...] = jnp.zeros(ys_ref.shape, U32)


def _experts(xs, wg, wu, wd, cap):
    E, L, W = xs.shape
    D = wg.shape[1]
    last = -(-cap // EXPERT_TILE) - 1
    wblk = lambda e, c: (e, 0, 0)
    return pl.pallas_call(
        functools.partial(_experts_kernel, cap=cap),
        grid=(E, L // EXPERT_TILE),
        in_specs=[
            pl.BlockSpec((None, EXPERT_TILE, W), lambda e, c: (e, jnp.minimum(c, last), 0)),
            pl.BlockSpec((None, D, D), wblk),
            pl.BlockSpec((None, D, D), wblk),
            pl.BlockSpec((None, D, D), wblk),
        ],
        out_specs=pl.BlockSpec((None, EXPERT_TILE, W), lambda e, c: (e, c, 0)),
        out_shape=jax.ShapeDtypeStruct((E, L, W), U32),
        scratch_shapes=[pltpu.VMEM((D, D), BF16), pltpu.VMEM((D, D), BF16), pltpu.VMEM((D, D), BF16)],
        compiler_params=_params(("arbitrary", "arbitrary")),
        name="experts",
    )(xs, wg, wu, wd)


def _combine_kernel(off_s, maxn_s, x1_ref, gsel_ref, tri_ref, ys_ref, y_ref, in_buf, sems, *, NB, TB):
    def start_copies(jj):
        def fn(slot):
            for cp in _slot_copies(jj, NB, off_s, slot, in_buf.at[jj % 2], ys_ref, sems.at[jj % 2], False):
                cp.start()
        _on_slot(maxn_s, jj, TB, fn)

    def wait_copies(jj):
        def fn(slot):
            for cp in _slot_copies(jj, NB, off_s, slot, in_buf.at[jj % 2], ys_ref, sems.at[jj % 2], False):
                cp.wait()
        _on_slot(maxn_s, jj, TB, fn)

    def block(j, rows):
        buf = in_buf.at[j % 2]

        @pl.when(j == 0)
        def _():
            start_copies(j)

        @pl.when(j + 1 < NB)
        def _():
            start_copies(jnp.minimum(j + 1, NB - 1))

        _, gate, pos = _picks(gsel_ref[:, rows], tri_ref)
        leads = _leads(off_s, j, NB)
        wait_copies(j)

        def compute(slot):
            acc = x1_ref[rows, :]
            for c in range(N_EXPERTS * slot // MOE_BLOCK):
                pc = _onehot_chunk(c, slot, pos, gate, leads)
                yc = _unpack_pairs(buf[c * MOE_BLOCK:(c + 1) * MOE_BLOCK, :])
                acc = acc + lax.dot_general(pc, yc, (((0,), (0,)), ((), ())), preferred_element_type=F32)
            y_ref[rows, :] = acc

        _on_slot(maxn_s, j, TB, compute)

    per_step = x1_ref.shape[0] // TB
    for s in range(per_step):
        block(pl.program_id(0) * per_step + s, slice(s * TB, (s + 1) * TB))


def _combine(tables, x1, gsel, ys):
    _, off, maxn = tables
    T = x1.shape[0]
    TB = min(MOE_BLOCK, T)
    NB = T // TB
    rows = _blocks_per_step(NB) * TB
    grid_spec = pltpu.PrefetchScalarGridSpec(
        num_scalar_prefetch=2,
        grid=(T // rows,),
        in_specs=[
            pl.BlockSpec((rows, D_MODEL), lambda j, *_: (j, 0)),
            pl.BlockSpec((N_EXPERTS, rows), lambda j, *_: (0, j)),
            pl.BlockSpec((TB, TB), lambda j, *_: (0, 0)),
            pl.BlockSpec(memory_space=pl.ANY),
        ],
        out_specs=pl.BlockSpec((rows, D_MODEL), lambda j, *_: (j, 0)),
        scratch_shapes=[
            pltpu.VMEM((2, N_EXPERTS * (TB + SLOT_SIZES[-1]), PACKED_W), U32),
            pltpu.SemaphoreType.DMA((2,)),
        ],
    )
    return pl.pallas_call(
        functools.partial(_combine_kernel, NB=NB, TB=TB),
        grid_spec=grid_spec,
        out_shape=jax.ShapeDtypeStruct((T, D_MODEL), F32),
        compiler_params=_params(("arbitrary",)),
        name="combine",
    )(off, maxn, x1, gsel, _strict_upper(TB), ys)


def _rope_tables(seq):
    pos = jnp.arange(seq, dtype=I32)
    row = (pos // GRID_W).astype(F32)
    col = (pos % GRID_W).astype(F32)
    axis_dim = HEAD_DIM // 2
    inv_freq = ROPE_THETA ** (-jnp.arange(0, axis_dim, 2, dtype=F32) / axis_dim)
    ra = row[:, None] * inv_freq[None, :]
    ca = col[:, None] * inv_freq[None, :]
    cos = jnp.concatenate([jnp.cos(ra), jnp.cos(ra), jnp.cos(ca), jnp.cos(ca)], axis=-1)
    sin = jnp.concatenate([-jnp.sin(ra), jnp.sin(ra), -jnp.sin(ca), jnp.sin(ca)], axis=-1)
    reps = LANES // HEAD_DIM
    return jnp.tile(cos, (1, reps)), jnp.tile(sin, (1, reps))


def _head_mean_matrix(width):
    r = lax.broadcasted_iota(I32, (width, width), 0) // HEAD_DIM
    c = lax.broadcasted_iota(I32, (width, width), 1) // HEAD_DIM
    return jnp.where(r == c, 1.0 / HEAD_DIM, 0.0).astype(BF16)


def _block_diag(w):
    n, d, _ = w.shape
    eye = jnp.eye(n, dtype=w.dtype)
    return jnp.einsum('nde,nm->ndme', w, eye).reshape(n * d, n * d)


def _group_gate_weights(w_r, w_i):
    per = LANES // LRU_BLOCK_DIM
    groups = [jnp.concatenate([_block_diag(w_r[g * per:(g + 1) * per]),
                               _block_diag(w_i[g * per:(g + 1) * per])], axis=1)
              for g in range(LRU_GROUPS)]
    return jnp.stack(groups).astype(BF16)


def _layer(x, weights):
    (g1, w_in, qg, kg, cw, cb, wf, wb, bias, lam, wa, wl, g2, rwt, wg, wu, wd) = weights
    B, S, D = x.shape
    T = B * S
    x2d = x.reshape(T, D)
    cos, sin = _rope_tables(S)
    hq = _head_mean_matrix(ATTN_W)
    hk = _head_mean_matrix(KV_W)
    q, ka, kb, v, xr, ga = _proj(x2d, S, g1, w_in, qg, kg, cos, sin, hq, hk)
    lru = _lru(xr.reshape(B, S, LRU_W), ga.reshape(B, S, LRU_W), cw, cb, wf, wb, bias, lam)
    att = _attn(q.reshape(B, S, ATTN_W), ka.reshape(B, S, KV_W), kb.reshape(B, S, KV_W),
                v.reshape(B, S, KV_W))
    x1, h2, aff_t = _outproj(att.reshape(T, ATTN_W), lru.reshape(T, LRU_W), x2d, wa, wl, g2, rwt)

    cap = max(1, EXPERT_CAPACITY_FACTOR * T // N_EXPERTS)
    TB = min(MOE_BLOCK, T)
    NB = T // TB
    L = -(-(cap + TB + SLOT_SIZES[-1]) // EXPERT_TILE) * EXPERT_TILE
    gsel, cnt, off, maxn = _select(aff_t, cap, NB)
    tables = (cnt.reshape(-1), off.reshape(-1), maxn[0])
    xs = _gather(tables, h2, gsel, L, cap)
    ys = _experts(xs, wg, wu, wd, cap)
    y = _combine(tables, x1, gsel, ys)
    return y.reshape(B, S, D)


def kernel(x_prompt, x_sample, norm1_g, w_in, q_norm_g, k_norm_g, conv_w, conv_b, lru_w_r, lru_b_r,
           lru_w_i, lru_b_i, lru_lambda, w_out, norm2_g, router_w, exp_w_gate, exp_w_up, exp_w_down):
    layers = []
    for l in range(norm1_g.shape[0]):
        bias = jnp.concatenate([lru_b_r[l, 0], lru_b_i[l, 0], lru_b_r[l, 1], lru_b_i[l, 1]])[None, :]
        layers.append((
            norm1_g[l][None, :], w_in[l].astype(BF16),
            jnp.tile(q_norm_g[l], N_Q_HEADS)[None, :], jnp.tile(k_norm_g[l], N_KV_HEADS)[None, :],
            conv_w[l], conv_b[l][None, :],
            _group_gate_weights(lru_w_r[l, 0], lru_w_i[l, 0]), _group_gate_weights(lru_w_r[l, 1], lru_w_i[l, 1]),
            bias, lru_lambda[l],
            w_out[l, :ATTN_W].astype(BF16), w_out[l, ATTN_W:].astype(BF16),
            norm2_g[l][None, :], router_w[l].T.astype(BF16),
            exp_w_gate[l], exp_w_up[l], exp_w_down[l],
        ))
    ys = []
    for x in (x_prompt, x_sample):
        for weights in layers:
            x = _layer(x, weights)
        ys.append(x)
    return tuple(ys)
```

```python
import functools

import jax
import jax.numpy as jnp
from jax import lax
from jax.experimental import pallas as pl
from jax.experimental.pallas import tpu as pltpu

F32 = jnp.float32
BF16 = jnp.bfloat16
I32 = jnp.int32
U32 = jnp.uint32

D_MODEL = 1024
GRID_W = 64
HEAD_DIM = 64
N_Q_HEADS = 8
N_KV_HEADS = 2
ATTN_W = N_Q_HEADS * HEAD_DIM
KV_W = N_KV_HEADS * HEAD_DIM
LRU_W = D_MODEL - ATTN_W
LRU_BLOCKS = 8
LRU_BLOCK_DIM = LRU_W // LRU_BLOCKS
CONV_W = 4
LRU_C = 8.0
N_EXPERTS = 16
EXPERT_CAPACITY_FACTOR = 2
ROPE_THETA = 10000.0
EPS = 1e-6
IN_W = ATTN_W + 2 * KV_W + 2 * LRU_W
LOG2E = 1.4426950408889634

LANES = 128
SUBLANES = 8
MXU_TILE = 256
VMEM_LIMIT = 56 * 1024 * 1024

TOKEN_TILE = 1024
Q_TILE = 512
LRU_CHUNK = 512
LRU_GROUPS = LRU_W // LANES
MOE_BLOCK = 256
GATHER_ROWS = 1024
EXPERT_TILE = 1024
ZERO_ROWS = 256
SLOT_SIZES = (48, 64)
P_CHUNK = 16
PACKED_W = D_MODEL // 2


def _params(sem, vmem=VMEM_LIMIT):
    return pltpu.CompilerParams(dimension_semantics=sem, vmem_limit_bytes=vmem)


def _swap16(x, lane):
    fwd = pltpu.roll(x, 16, axis=1)
    bwd = pltpu.roll(x, LANES - 16, axis=1)
    return jnp.where((lane & 16) == 0, bwd, fwd)


def _split_dot(x, w):
    hi = x.astype(BF16)
    lo = (x - hi.astype(F32)).astype(BF16)
    return (jnp.dot(hi, w, preferred_element_type=F32)
            + jnp.dot(lo, w, preferred_element_type=F32))


def _proj_kernel(x_ref, g1_ref, w_ref, qg_ref, kg_ref, cos_ref, sin_ref, hq_ref, hk_ref,
                 q_ref, ka_ref, kb_ref, v_ref, xr_ref, ga_ref):
    x = x_ref[...]
    ms = jnp.mean(x * x, axis=-1, keepdims=True)
    h = (x * lax.rsqrt(ms + EPS) * g1_ref[...]).astype(BF16)
    proj = jnp.dot(h, w_ref[...], preferred_element_type=F32)
    q = proj[:, :ATTN_W]
    k = proj[:, ATTN_W:ATTN_W + KV_W]
    v = proj[:, ATTN_W + KV_W:ATTN_W + 2 * KV_W]
    xr = proj[:, ATTN_W + 2 * KV_W:ATTN_W + 2 * KV_W + LRU_W]
    gate = proj[:, ATTN_W + 2 * KV_W + LRU_W:]

    cos = cos_ref[...]
    sin = sin_ref[...]
    lane = lax.broadcasted_iota(I32, cos.shape, 1)

    qq = q * q
    half = hq_ref.shape[0]
    q_ms = jnp.concatenate([_split_dot(qq[:, c:c + half], hq_ref[...]) for c in range(0, ATTN_W, half)], axis=1)
    qn = q * lax.rsqrt(q_ms + EPS) * qg_ref[...]
    kn = k * lax.rsqrt(_split_dot(k * k, hk_ref[...]) + EPS) * kg_ref[...]

    qscale = (HEAD_DIM ** -0.5) * LOG2E
    for p in range(ATTN_W // LANES):
        qc = qn[:, p * LANES:(p + 1) * LANES]
        qr = qc * cos + _swap16(qc, lane) * sin
        q_ref[:, p * LANES:(p + 1) * LANES] = (qr * qscale).astype(BF16)
    kr = kn * cos + _swap16(kn, lane) * sin
    ka_ref[...] = kr.astype(BF16)
    kb_ref[...] = pltpu.roll(kr, HEAD_DIM, axis=1).astype(BF16)
    v_ref[...] = v.astype(BF16)
    xr_ref[...] = xr
    ga_ref[...] = 0.5 * gate * (1.0 + jnp.tanh(0.7978845608028654 * (gate + 0.044715 * gate * gate * gate)))


def _proj(x2d, seq, g1, w_in, qg, kg, cos, sin, hq, hk):
    T = x2d.shape[0]
    tm = min(TOKEN_TILE, seq)
    nseq = seq // tm
    row = lambda i: (i, 0)
    const = lambda i: (0, 0)
    pos = lambda i: (i % nseq, 0)
    out_shapes = (
        jax.ShapeDtypeStruct((T, ATTN_W), BF16),
        jax.ShapeDtypeStruct((T, KV_W), BF16),
        jax.ShapeDtypeStruct((T, KV_W), BF16),
        jax.ShapeDtypeStruct((T, KV_W), BF16),
        jax.ShapeDtypeStruct((T, LRU_W), F32),
        jax.ShapeDtypeStruct((T, LRU_W), F32),
    )
    return pl.pallas_call(
        _proj_kernel,
        grid=(T // tm,),
        in_specs=[
            pl.BlockSpec((tm, D_MODEL), row),
            pl.BlockSpec((1, D_MODEL), const),
            pl.BlockSpec((D_MODEL, IN_W), const),
            pl.BlockSpec((1, ATTN_W), const),
            pl.BlockSpec((1, KV_W), const),
            pl.BlockSpec((tm, LANES), pos),
            pl.BlockSpec((tm, LANES), pos),
            pl.BlockSpec((MXU_TILE, MXU_TILE), const),
            pl.BlockSpec((KV_W, KV_W), const),
        ],
        out_specs=(
            pl.BlockSpec((tm, ATTN_W), row),
            pl.BlockSpec((tm, KV_W), row),
            pl.BlockSpec((tm, KV_W), row),
            pl.BlockSpec((tm, KV_W), row),
            pl.BlockSpec((tm, LRU_W), row),
            pl.BlockSpec((tm, LRU_W), row),
        ),
        out_shape=out_shapes,
        compiler_params=_params(("parallel",)),
        name="proj",
    )(x2d, g1, w_in, qg, kg, cos, sin, hq, hk)


def _sigmoid(x):
    return 0.5 * jnp.tanh(0.5 * x) + 0.5


def _scan8(a, b, h_in, first_row, shifts):
    b = b + jnp.where(first_row, a * h_in, 0.0)
    a = jnp.where(first_row, 0.0, a)
    for d in shifts:
        b = b + a * pltpu.roll(b, d, axis=0)
        a = a * pltpu.roll(a, d, axis=0)
    return b


def _lru_kernel(xr_ref, ga_ref, cw_ref, cb_ref, wf_ref, wb_ref, bias_ref, lam_ref, o_ref,
                xpad_s, xc_s, hf_s, hb_s, pre_f, pre_b):
    S = xr_ref.shape[0]
    C = LRU_W
    CH = min(LRU_CHUNK, S)
    NC = S // CH
    G = CH // SUBLANES
    left = CONV_W // 2

    xpad_s[0:SUBLANES, :] = jnp.zeros((SUBLANES, C), F32)
    xpad_s[SUBLANES + S:2 * SUBLANES + S, :] = jnp.zeros((SUBLANES, C), F32)
    xpad_s[SUBLANES:SUBLANES + S, :] = xr_ref[...]
    cw_half = 0.5 * cw_ref[...]
    cb_half = 0.5 * cb_ref[...]
    for c in range(NC):
        u = xpad_s[c * CH:c * CH + CH + 2 * SUBLANES, :]
        acc = cb_half + jnp.zeros((CH, C), F32)
        for tap in range(CONV_W):
            shift = (left - tap) % (CH + 2 * SUBLANES)
            ush = u if shift == 0 else pltpu.roll(u, shift, axis=0)
            acc = acc + ush[SUBLANES:SUBLANES + CH, :] * cw_half[tap:tap + 1, :]
        xc_s[c * CH:(c + 1) * CH, :] = acc

    z = -lam_ref[...]
    c1 = (-0.5 * LRU_C * LOG2E) * (jnp.maximum(z, 0.0) + jnp.log1p(jnp.exp(-jnp.abs(z))))
    c1_f = c1[0:1, :]
    c1_b = c1[1:2, :]
    bias_half = 0.5 * bias_ref[...]
    row = lax.broadcasted_iota(I32, (SUBLANES, C), 0)
    is_first = row == 0
    is_last = row == SUBLANES - 1

    def gate_preacts(x_bf, w_ref, b0, pre_ref):
        for g in range(LRU_GROUPS):
            res = jnp.dot(x_bf[:, g * LANES:(g + 1) * LANES], w_ref[g], preferred_element_type=F32)
            pre_ref[:, g * LANES:(g + 1) * LANES] = (
                res[:, :LANES] + bias_half[:, b0 + g * LANES:b0 + (g + 1) * LANES])
            pre_ref[:, C + g * LANES:C + (g + 1) * LANES] = (
                res[:, LANES:] + bias_half[:, b0 + C + g * LANES:b0 + C + (g + 1) * LANES])

    def gates(pre_ref, g, x8_half, c1_d):
        tr = jnp.tanh(pre_ref[pl.ds(g * SUBLANES, SUBLANES), 0:C])
        ti = jnp.tanh(pre_ref[pl.ds(g * SUBLANES, SUBLANES), C:2 * C])
        a = jnp.exp2(c1_d * tr + c1_d)
        y = 1.0 - a * a
        mult = jnp.where(y > 0.0, y * lax.rsqrt(y), 0.0)
        return a, mult * ((ti + 1.0) * x8_half)

    hf = jnp.zeros((SUBLANES, C), F32)
    hb = jnp.zeros((SUBLANES, C), F32)
    for c in range(NC):
        cf = c
        cb = NC - 1 - c
        gate_preacts(xc_s[cf * CH:(cf + 1) * CH, :].astype(BF16), wf_ref, 0, pre_f)
        gate_preacts(xc_s[cb * CH:(cb + 1) * CH, :].astype(BF16), wb_ref, 2 * C, pre_b)

        def body(g, carry, cf=cf, cb=cb):
            hf, hb = carry
            rf = pl.multiple_of(cf * CH + g * SUBLANES, SUBLANES)
            a, b = gates(pre_f, g, xc_s[pl.ds(rf, SUBLANES), :], c1_f)
            h = _scan8(a, b, hf, is_first, (1, 2, 4))
            hf_s[pl.ds(rf, SUBLANES), :] = h
            hf = jnp.broadcast_to(h[SUBLANES - 1:SUBLANES, :], (SUBLANES, C))
            gb = G - 1 - g
            rb = pl.multiple_of(cb * CH + gb * SUBLANES, SUBLANES)
            a, b = gates(pre_b, gb, xc_s[pl.ds(rb, SUBLANES), :], c1_b)
            h = _scan8(a, b, hb, is_last, (SUBLANES - 1, SUBLANES - 2, SUBLANES - 4))
            hb_s[pl.ds(rb, SUBLANES), :] = h
            hb = jnp.broadcast_to(h[0:1, :], (SUBLANES, C))
            return hf, hb

        hf, hb = lax.fori_loop(0, G, body, (hf, hb), unroll=2)

    o_ref[...] = ((hf_s[...] + hb_s[...]) * ga_ref[...]).astype(BF16)


def _lru(xr, ga, cw, cb, wf, wb, bias, lam):
    B, S, C = xr.shape
    CH = min(LRU_CHUNK, S)
    blk = lambda b: (b, 0, 0)
    const2 = lambda b: (0, 0)
    const3 = lambda b: (0, 0, 0)
    return pl.pallas_call(
        _lru_kernel,
        grid=(B,),
        in_specs=[
            pl.BlockSpec((None, S, C), blk),
            pl.BlockSpec((None, S, C), blk),
            pl.BlockSpec((CONV_W, C), const2),
            pl.BlockSpec((1, C), const2),
            pl.BlockSpec((LRU_GROUPS, LANES, 2 * LANES), const3),
            pl.BlockSpec((LRU_GROUPS, LANES, 2 * LANES), const3),
            pl.BlockSpec((1, 4 * C), const2),
            pl.BlockSpec((2, C), const2),
        ],
        out_specs=pl.BlockSpec((None, S, C), blk),
        out_shape=jax.ShapeDtypeStruct((B, S, C), BF16),
        scratch_shapes=[
            pltpu.VMEM((S + 2 * SUBLANES, C), F32),
            pltpu.VMEM((S, C), F32),
            pltpu.VMEM((S, C), F32),
            pltpu.VMEM((S, C), F32),
            pltpu.VMEM((CH, 2 * C), F32),
            pltpu.VMEM((CH, 2 * C), F32),
        ],
        compiler_params=_params(("parallel",)),
        name="lru",
    )(xr, ga, cw, cb, wf, wb, bias, lam)


ONES_ROWS = 16


def _attn_kernel(q_ref, ka_ref, kb_ref, v_ref, o_ref, vt_s):
    tq = q_ref.shape[0]
    S = ka_ref.shape[0]

    @pl.when(pl.program_id(1) == 0)
    def _():
        vt = v_ref[...].astype(F32).T.astype(BF16)
        for kv in range(N_KV_HEADS):
            vt_s[kv, 0:HEAD_DIM, :] = vt[kv * HEAD_DIM:(kv + 1) * HEAD_DIM, :]
            vt_s[kv, HEAD_DIM:HEAD_DIM + ONES_ROWS, :] = jnp.ones((ONES_ROWS, S), BF16)

    lane = lax.broadcasted_iota(I32, (tq, LANES), 1)
    low = lane < HEAD_DIM
    contract_last = (((1,), (1,)), ((), ()))
    q_per_kv = N_Q_HEADS // N_KV_HEADS
    heads = [(p, half) for p in range(ATTN_W // LANES) for half in range(2)]

    def scores(p, half):
        qp = q_ref[:, p * LANES:(p + 1) * LANES]
        qm = jnp.where(low if half == 0 else jnp.logical_not(low), qp, jnp.zeros_like(qp))
        k_ref = ka_ref if (2 * p) // q_per_kv == half else kb_ref
        st = lax.dot_general(k_ref[...], qm, contract_last, preferred_element_type=F32)
        return st.astype(BF16)

    halves = []
    ahead = [scores(*heads[0]), scores(*heads[1])]
    for i, (p, half) in enumerate(heads):
        sb = ahead.pop(0)
        if i + 2 < len(heads):
            ahead.append(scores(*heads[i + 2]))
        kv = (2 * p) // q_per_kv
        pt = jnp.exp2(sb - jnp.max(sb, axis=0, keepdims=True))
        ot = jnp.dot(vt_s[kv], pt, preferred_element_type=F32)
        halves.append(ot[:HEAD_DIM, :] / ot[HEAD_DIM:HEAD_DIM + 1, :])
        if half == 1:
            pair_t = jnp.concatenate(halves, axis=0)
            o_ref[:, p * LANES:(p + 1) * LANES] = pair_t.T.astype(BF16)
            halves = []


def _attn(q, ka, kb, v):
    B, S, _ = q.shape
    tq = min(Q_TILE, S)
    qblk = lambda b, i: (b, i, 0)
    kblk = lambda b, i: (b, 0, 0)
    return pl.pallas_call(
        _attn_kernel,
        grid=(B, S // tq),
        in_specs=[
            pl.BlockSpec((None, tq, ATTN_W), qblk),
            pl.BlockSpec((None, S, KV_W), kblk),
            pl.BlockSpec((None, S, KV_W), kblk),
            pl.BlockSpec((None, S, KV_W), kblk),
        ],
        out_specs=pl.BlockSpec((None, tq, ATTN_W), qblk),
        out_shape=jax.ShapeDtypeStruct((B, S, ATTN_W), BF16),
        scratch_shapes=[pltpu.VMEM((N_KV_HEADS, HEAD_DIM + ONES_ROWS, S), BF16)],
        compiler_params=_params(("parallel", "arbitrary")),
        name="attn",
    )(q, ka, kb, v)


def _outproj_kernel(a_ref, l_ref, x_ref, wa_ref, wl_ref, g2_ref, rw_ref, x1_ref, h2_ref, aff_ref):
    x1 = (x_ref[...]
          + jnp.dot(a_ref[...], wa_ref[...], preferred_element_type=F32)
          + jnp.dot(l_ref[...], wl_ref[...], preferred_element_type=F32))
    x1_ref[...] = x1
    ms = jnp.mean(x1 * x1, axis=-1, keepdims=True)
    h2 = (x1 * lax.rsqrt(ms + EPS) * g2_ref[...]).astype(BF16)
    h2_ref[...] = h2
    logits = lax.dot_general(rw_ref[...], h2, (((1,), (1,)), ((), ())), preferred_element_type=F32)
    m = jnp.max(logits, axis=0, keepdims=True)
    e = jnp.exp(logits - m)
    aff_ref[...] = e / jnp.sum(e, axis=0, keepdims=True)


def _outproj(attn2d, lru2d, x2d, wa, wl, g2, rwt):
    T = x2d.shape[0]
    tm = min(TOKEN_TILE, T)
    row = lambda i: (i, 0)
    const = lambda i: (0, 0)
    return pl.pallas_call(
        _outproj_kernel,
        grid=(T // tm,),
        in_specs=[
            pl.BlockSpec((tm, ATTN_W), row),
            pl.BlockSpec((tm, LRU_W), row),
            pl.BlockSpec((tm, D_MODEL), row),
            pl.BlockSpec((ATTN_W, D_MODEL), const),
            pl.BlockSpec((LRU_W, D_MODEL), const),
            pl.BlockSpec((1, D_MODEL), const),
            pl.BlockSpec((N_EXPERTS, D_MODEL), const),
        ],
        out_specs=(
            pl.BlockSpec((tm, D_MODEL), row),
            pl.BlockSpec((tm, D_MODEL), row),
            pl.BlockSpec((N_EXPERTS, tm), lambda i: (0, i)),
        ),
        out_shape=(
            jax.ShapeDtypeStruct((T, D_MODEL), F32),
            jax.ShapeDtypeStruct((T, D_MODEL), BF16),
            jax.ShapeDtypeStruct((N_EXPERTS, T), F32),
        ),
        compiler_params=_params(("parallel",)),
        name="outproj",
    )(attn2d, lru2d, x2d, wa, wl, g2, rwt)


def _select_kernel(aff_ref, tri_ref, trinb_ref, gsel_ref, cnt_ref, off_ref, maxn_ref, *, cap, T, NB):
    TB = T // NB
    CW = min(2048, T)
    E = N_EXPERTS

    def count_where(pred):
        def body(i, acc):
            bits = pltpu.bitcast(aff_ref[:, pl.ds(pl.multiple_of(i * CW, LANES), CW)], I32)
            m = pred(bits).astype(I32)
            part = m[:, 0:LANES]
            for qd in range(1, CW // LANES):
                part = part + m[:, qd * LANES:(qd + 1) * LANES]
            return acc + part
        acc = lax.fori_loop(0, T // CW, body, jnp.zeros((E, LANES), I32))
        return jnp.sum(acc, axis=1, keepdims=True)

    def bit_step(it, thr):
        cand = thr | jnp.left_shift(jnp.int32(1), 30 - it)
        n = count_where(lambda bits: bits >= cand)
        return jnp.where(n >= cap, cand, thr)

    thr = lax.fori_loop(0, 31, bit_step, jnp.zeros((E, 1), I32))
    n_gt = count_where(lambda bits: bits > thr)
    need = (cap - n_gt).astype(F32)

    lane_nb = lax.broadcasted_iota(I32, (E, NB), 1)

    def blk_step(j, carry):
        eq_before, cnt_acc = carry
        aff = aff_ref[:, pl.ds(pl.multiple_of(j * TB, LANES), TB)]
        bits = pltpu.bitcast(aff, I32)
        gt = bits > thr
        eq = bits == thr
        eqf = jnp.where(eq, 1.0, 0.0)
        pref = jnp.dot(eqf.astype(BF16), tri_ref[...], preferred_element_type=F32)
        take = jnp.logical_and(eq, (eq_before + pref) < need)
        sel = jnp.logical_or(gt, take)
        gsel_ref[:, pl.ds(pl.multiple_of(j * TB, LANES), TB)] = jnp.where(sel, aff, -1.0)
        cnt = jnp.sum(jnp.where(sel, 1.0, 0.0), axis=1, keepdims=True)
        cnt_acc = jnp.where(lane_nb == j, cnt, cnt_acc)
        return eq_before + jnp.sum(eqf, axis=1, keepdims=True), cnt_acc

    _, cnt = lax.fori_loop(0, NB, blk_step, (jnp.zeros((E, 1), F32), jnp.zeros((E, NB), F32)))
    off = jnp.dot(cnt.astype(BF16), trinb_ref[...], preferred_element_type=F32)
    cnt_ref[...] = cnt.astype(I32)
    off_ref[...] = off.astype(I32)
    maxn_ref[...] = jnp.broadcast_to(jnp.max(cnt, axis=0, keepdims=True), (SUBLANES, NB)).astype(I32)


def _strict_upper(n):
    r = lax.broadcasted_iota(I32, (n, n), 0)
    c = lax.broadcasted_iota(I32, (n, n), 1)
    return (r < c).astype(BF16)


def _select(aff_t, cap, NB):
    E, T = aff_t.shape
    TB = T // NB
    full = lambda shape: pl.BlockSpec(shape, lambda i: (0,) * len(shape))
    return pl.pallas_call(
        functools.partial(_select_kernel, cap=cap, T=T, NB=NB),
        grid=(1,),
        in_specs=[full((E, T)), full((TB, TB)), full((NB, NB))],
        out_specs=(full((E, T)), full((E, NB)), full((E, NB)), full((SUBLANES, NB))),
        out_shape=(
            jax.ShapeDtypeStruct((E, T), F32),
            jax.ShapeDtypeStruct((E, NB), I32),
            jax.ShapeDtypeStruct((E, NB), I32),
            jax.ShapeDtypeStruct((SUBLANES, NB), I32),
        ),
        compiler_params=_params(("arbitrary",)),
        name="select",
    )(aff_t, _strict_upper(TB), _strict_upper(NB))


def _pack_pairs(x):
    n = x.shape[1] // 2
    lo = pltpu.bitcast(x[:, :n].astype(BF16).astype(F32), U32)
    hi = pltpu.bitcast(x[:, n:].astype(BF16).astype(F32), U32)
    return hi | (lo >> 16)


def _unpack_pairs(w):
    lo = pltpu.bitcast(w << 16, F32)
    hi = pltpu.bitcast(w & jnp.uint32(0xFFFF0000), F32)
    return jnp.concatenate([lo, hi], axis=1).astype(BF16)


def _window(off_s, j, NB, e):
    off = off_s[e * NB + j]
    lead = off & (SUBLANES - 1)
    return off - lead, lead


def _onehot_chunk(c, slot, pos, val, leads):
    tokens = pos.shape[1]
    pieces = []
    for p in range(MOE_BLOCK // P_CHUNK):
        row0 = c * MOE_BLOCK + p * P_CHUNK
        e, s0 = row0 // slot, row0 % slot
        s = (s0 + lax.broadcasted_iota(I32, (P_CHUNK, tokens), 0)).astype(F32)
        pieces.append(jnp.where(pos[e:e + 1, :] + leads[e] == s, val[e:e + 1, :], 0.0))
    return jnp.concatenate(pieces, axis=0).astype(BF16)


def _slot_copies(j, NB, off_s, slot, buf, lst, sem, to_list):
    copies = []
    for e in range(N_EXPERTS):
        start, _ = _window(off_s, j, NB, e)
        b = buf.at[e * slot:(e + 1) * slot]
        l = lst.at[e, pl.ds(pl.multiple_of(start, SUBLANES), slot)]
        copies.append(pltpu.make_async_copy(b, l, sem) if to_list else pltpu.make_async_copy(l, b, sem))
    return copies


def _leads(off_s, j, NB):
    return [_window(off_s, j, NB, e)[1].astype(F32) for e in range(N_EXPERTS)]


def _blocks_per_step(NB):
    return 1


def _picks(g, tri_ref):
    sel = g >= 0.0
    self_ = jnp.where(sel, 1.0, 0.0)
    pos = jnp.dot(self_.astype(BF16), tri_ref[...], preferred_element_type=F32)
    return self_, jnp.where(sel, g, 0.0), pos


def _on_slot(maxn_s, jj, tokens, fn):
    def pick(sizes):
        if not sizes:
            return lambda: fn(tokens + SLOT_SIZES[-1])
        fits = maxn_s[jj] + (SUBLANES - 1) <= sizes[0]
        return lambda: lax.cond(fits, lambda: fn(sizes[0]), pick(sizes[1:]))
    pick([s for s in SLOT_SIZES if s < tokens])()


def _gather_kernel(cnt_s, off_s, maxn_s, h2_ref, gsel_ref, tri_ref, list_ref, out_buf, carry_s, zero_buf,
                   sems, *, NB, TB, cap):
    L = list_ref.shape[1]

    def compute(j, rows, slot):
        buf = out_buf.at[j % 2]
        self_, _, pos = _picks(gsel_ref[:, rows], tri_ref)
        leads = _leads(off_s, j, NB)
        h2 = h2_ref[rows, :]
        total = N_EXPERTS * slot
        for r0 in range(0, total, GATHER_ROWS):
            r1 = min(r0 + GATHER_ROWS, total)
            onehot = jnp.concatenate(
                [_onehot_chunk(c, slot, pos, self_, leads)
                 for c in range(r0 // MOE_BLOCK, r1 // MOE_BLOCK)], axis=0)
            picked = jnp.dot(onehot, h2, preferred_element_type=F32)
            buf[r0:r1, :] = _pack_pairs(picked)
        row = lax.broadcasted_iota(I32, (SUBLANES, PACKED_W), 0)
        for e in range(N_EXPERTS):
            _, lead = _window(off_s, j, NB, e)
            head = buf[e * slot:e * slot + SUBLANES, :]
            buf[e * slot:e * slot + SUBLANES, :] = jnp.where(row < lead, carry_s[e], head)
            filled = lead + cnt_s[e * NB + j]
            last = jnp.minimum((filled // SUBLANES) * SUBLANES, slot - SUBLANES)
            carry_s[e] = buf[pl.ds(pl.multiple_of(e * slot + last, SUBLANES), SUBLANES), :]

    def start_copies(jj):
        def fn(slot):
            for cp in _slot_copies(jj, NB, off_s, slot, out_buf.at[jj % 2], list_ref, sems.at[jj % 2], True):
                cp.start()
        _on_slot(maxn_s, jj, TB, fn)

    def wait_copies(jj):
        def fn(slot):
            for cp in _slot_copies(jj, NB, off_s, slot, out_buf.at[jj % 2], list_ref, sems.at[jj % 2], True):
                cp.wait()
        _on_slot(maxn_s, jj, TB, fn)

    def block(j, rows):
        @pl.when(j == 0)
        def _():
            carry_s[...] = jnp.zeros(carry_s.shape, U32)

        _on_slot(maxn_s, j, TB, functools.partial(compute, j, rows))

        @pl.when(j > 0)
        def _():
            wait_copies(jnp.maximum(j - 1, 0))

        start_copies(j)

        @pl.when(j == NB - 1)
        def _():
            wait_copies(j)
            zero_buf[...] = jnp.zeros(zero_buf.shape, U32)
            sem = sems.at[0]
            for r in range(-(-(L - cap) // ZERO_ROWS)):
                start = min(cap + r * ZERO_ROWS, L - ZERO_ROWS)
                copies = [pltpu.make_async_copy(zero_buf, list_ref.at[e, start:start + ZERO_ROWS], sem)
                          for e in range(N_EXPERTS)]
                for cp in copies:
                    cp.start()
                for cp in copies:
                    cp.wait()

    per_step = h2_ref.shape[0] // TB
    for s in range(per_step):
        block(pl.program_id(0) * per_step + s, slice(s * TB, (s + 1) * TB))


def _gather(tables, h2, gsel, L, cap):
    T = h2.shape[0]
    TB = min(MOE_BLOCK, T)
    NB = T // TB
    rows = _blocks_per_step(NB) * TB
    grid_spec = pltpu.PrefetchScalarGridSpec(
        num_scalar_prefetch=3,
        grid=(T // rows,),
        in_specs=[
            pl.BlockSpec((rows, D_MODEL), lambda j, *_: (j, 0)),
            pl.BlockSpec((N_EXPERTS, rows), lambda j, *_: (0, j)),
            pl.BlockSpec((TB, TB), lambda j, *_: (0, 0)),
        ],
        out_specs=pl.BlockSpec(memory_space=pl.ANY),
        scratch_shapes=[
            pltpu.VMEM((2, N_EXPERTS * (TB + SLOT_SIZES[-1]), PACKED_W), U32),
            pltpu.VMEM((N_EXPERTS, SUBLANES, PACKED_W), U32),
            pltpu.VMEM((ZERO_ROWS, PACKED_W), U32),
            pltpu.SemaphoreType.DMA((2,)),
        ],
    )
    return pl.pallas_call(
        functools.partial(_gather_kernel, NB=NB, TB=TB, cap=cap),
        grid_spec=grid_spec,
        out_shape=jax.ShapeDtypeStruct((N_EXPERTS, L, PACKED_W), U32),
        compiler_params=_params(("arbitrary",)),
        name="gather",
    )(*tables, h2, gsel, _strict_upper(TB))


def _experts_kernel(xs_ref, wg_ref, wu_ref, wd_ref, ys_ref, wg_s, wu_s, wd_s, *, cap):
    c = pl.program_id(1)
    live = c * EXPERT_TILE < cap

    @pl.when(c == 0)
    def _():
        wg_s[...] = wg_ref[...].astype(BF16)
        wu_s[...] = wu_ref[...].astype(BF16)
        wd_s[...] = wd_ref[...].astype(BF16)

    @pl.when(live)
    def _():
        xs = _unpack_pairs(xs_ref[...])
        hg = jnp.dot(xs, wg_s[...], preferred_element_type=F32)
        hu = jnp.dot(xs, wu_s[...], preferred_element_type=F32)
        hid = (hg * _sigmoid(hg) * hu).astype(BF16)
        ys_ref[...] = _pack_pairs(jnp.dot(hid, wd_s[...], preferred_element_type=F32))

    @pl.when(jnp.logical_not(live))
    def _():
        ys_ref[...] = jnp.zeros(ys_ref.shape, U32)


def _experts(xs, wg, wu, wd, cap):
    E, L, W = xs.shape
    D = wg.shape[1]
    last = -(-cap // EXPERT_TILE) - 1
    wblk = lambda e, c: (e, 0, 0)
    return pl.pallas_call(
        functools.partial(_experts_kernel, cap=cap),
        grid=(E, L // EXPERT_TILE),
        in_specs=[
            pl.BlockSpec((None, EXPERT_TILE, W), lambda e, c: (e, jnp.minimum(c, last), 0)),
            pl.BlockSpec((None, D, D), wblk),
            pl.BlockSpec((None, D, D), wblk),
            pl.BlockSpec((None, D, D), wblk),
        ],
        out_specs=pl.BlockSpec((None, EXPERT_TILE, W), lambda e, c: (e, c, 0)),
        out_shape=jax.ShapeDtypeStruct((E, L, W), U32),
        scratch_shapes=[pltpu.VMEM((D, D), BF16), pltpu.VMEM((D, D), BF16), pltpu.VMEM((D, D), BF16)],
        compiler_params=_params(("arbitrary", "arbitrary")),
        name="experts",
    )(xs, wg, wu, wd)


def _combine_kernel(off_s, maxn_s, x1_ref, gsel_ref, tri_ref, ys_ref, y_ref, in_buf, sems, *, NB, TB):
    def start_copies(jj):
        def fn(slot):
            for cp in _slot_copies(jj, NB, off_s, slot, in_buf.at[jj % 2], ys_ref, sems.at[jj % 2], False):
                cp.start()
        _on_slot(maxn_s, jj, TB, fn)

    def wait_copies(jj):
        def fn(slot):
            for cp in _slot_copies(jj, NB, off_s, slot, in_buf.at[jj % 2], ys_ref, sems.at[jj % 2], False):
                cp.wait()
        _on_slot(maxn_s, jj, TB, fn)

    def block(j, rows):
        buf = in_buf.at[j % 2]

        @pl.when(j == 0)
        def _():
            start_copies(j)

        @pl.when(j + 1 < NB)
        def _():
            start_copies(jnp.minimum(j + 1, NB - 1))

        _, gate, pos = _picks(gsel_ref[:, rows], tri_ref)
        leads = _leads(off_s, j, NB)
        wait_copies(j)

        def compute(slot):
            acc = x1_ref[rows, :]
            for c in range(N_EXPERTS * slot // MOE_BLOCK):
                pc = _onehot_chunk(c, slot, pos, gate, leads)
                yc = _unpack_pairs(buf[c * MOE_BLOCK:(c + 1) * MOE_BLOCK, :])
                acc = acc + lax.dot_general(pc, yc, (((0,), (0,)), ((), ())), preferred_element_type=F32)
            y_ref[rows, :] = acc

        _on_slot(maxn_s, j, TB, compute)

    per_step = x1_ref.shape[0] // TB
    for s in range(per_step):
        block(pl.program_id(0) * per_step + s, slice(s * TB, (s + 1) * TB))


def _combine(tables, x1, gsel, ys):
    _, off, maxn = tables
    T = x1.shape[0]
    TB = min(MOE_BLOCK, T)
    NB = T // TB
    rows = _blocks_per_step(NB) * TB
    grid_spec = pltpu.PrefetchScalarGridSpec(
        num_scalar_prefetch=2,
        grid=(T // rows,),
        in_specs=[
            pl.BlockSpec((rows, D_MODEL), lambda j, *_: (j, 0)),
            pl.BlockSpec((N_EXPERTS, rows), lambda j, *_: (0, j)),
            pl.BlockSpec((TB, TB), lambda j, *_: (0, 0)),
            pl.BlockSpec(memory_space=pl.ANY),
        ],
        out_specs=pl.BlockSpec((rows, D_MODEL), lambda j, *_: (j, 0)),
        scratch_shapes=[
            pltpu.VMEM((2, N_EXPERTS * (TB + SLOT_SIZES[-1]), PACKED_W), U32),
            pltpu.SemaphoreType.DMA((2,)),
        ],
    )
    return pl.pallas_call(
        functools.partial(_combine_kernel, NB=NB, TB=TB),
        grid_spec=grid_spec,
        out_shape=jax.ShapeDtypeStruct((T, D_MODEL), F32),
        compiler_params=_params(("arbitrary",)),
        name="combine",
    )(off, maxn, x1, gsel, _strict_upper(TB), ys)


def _rope_tables(seq):
    pos = jnp.arange(seq, dtype=I32)
    row = (pos // GRID_W).astype(F32)
    col = (pos % GRID_W).astype(F32)
    axis_dim = HEAD_DIM // 2
    inv_freq = ROPE_THETA ** (-jnp.arange(0, axis_dim, 2, dtype=F32) / axis_dim)
    ra = row[:, None] * inv_freq[None, :]
    ca = col[:, None] * inv_freq[None, :]
    cos = jnp.concatenate([jnp.cos(ra), jnp.cos(ra), jnp.cos(ca), jnp.cos(ca)], axis=-1)
    sin = jnp.concatenate([-jnp.sin(ra), jnp.sin(ra), -jnp.sin(ca), jnp.sin(ca)], axis=-1)
    reps = LANES // HEAD_DIM
    return jnp.tile(cos, (1, reps)), jnp.tile(sin, (1, reps))


def _head_mean_matrix(width):
    r = lax.broadcasted_iota(I32, (width, width), 0) // HEAD_DIM
    c = lax.broadcasted_iota(I32, (width, width), 1) // HEAD_DIM
    return jnp.where(r == c, 1.0 / HEAD_DIM, 0.0).astype(BF16)


def _block_diag(w):
    n, d, _ = w.shape
    eye = jnp.eye(n, dtype=w.dtype)
    return jnp.einsum('nde,nm->ndme', w, eye).reshape(n * d, n * d)


def _group_gate_weights(w_r, w_i):
    per = LANES // LRU_BLOCK_DIM
    groups = [jnp.concatenate([_block_diag(w_r[g * per:(g + 1) * per]),
                               _block_diag(w_i[g * per:(g + 1) * per])], axis=1)
              for g in range(LRU_GROUPS)]
    return jnp.stack(groups).astype(BF16)


def _layer(x, weights):
    (g1, w_in, qg, kg, cw, cb, wf, wb, bias, lam, wa, wl, g2, rwt, wg, wu, wd) = weights
    B, S, D = x.shape
    T = B * S
    x2d = x.reshape(T, D)
    cos, sin = _rope_tables(S)
    hq = _head_mean_matrix(MXU_TILE)
    hk = _head_mean_matrix(KV_W)
    q, ka, kb, v, xr, ga = _proj(x2d, S, g1, w_in, qg, kg, cos, sin, hq, hk)
    lru = _lru(xr.reshape(B, S, LRU_W), ga.reshape(B, S, LRU_W), cw, cb, wf, wb, bias, lam)
    att = _attn(q.reshape(B, S, ATTN_W), ka.reshape(B, S, KV_W), kb.reshape(B, S, KV_W),
                v.reshape(B, S, KV_W))
    x1, h2, aff_t = _outproj(att.reshape(T, ATTN_W), lru.reshape(T, LRU_W), x2d, wa, wl, g2, rwt)

    cap = max(1, EXPERT_CAPACITY_FACTOR * T // N_EXPERTS)
    TB = min(MOE_BLOCK, T)
    NB = T // TB
    L = -(-(cap + TB + SLOT_SIZES[-1]) // EXPERT_TILE) * EXPERT_TILE
    gsel, cnt, off, maxn = _select(aff_t, cap, NB)
    tables = (cnt.reshape(-1), off.reshape(-1), maxn[0])
    xs = _gather(tables, h2, gsel, L, cap)
    ys = _experts(xs, wg, wu, wd, cap)
    y = _combine(tables, x1, gsel, ys)
    return y.reshape(B, S, D)


def kernel(x_prompt, x_sample, norm1_g, w_in, q_norm_g, k_norm_g, conv_w, conv_b, lru_w_r, lru_b_r,
           lru_w_i, lru_b_i, lru_lambda, w_out, norm2_g, router_w, exp_w_gate, exp_w_up, exp_w_down):
    layers = []
    for l in range(norm1_g.shape[0]):
        bias = jnp.concatenate([lru_b_r[l, 0], lru_b_i[l, 0], lru_b_r[l, 1], lru_b_i[l, 1]])[None, :]
        layers.append((
            norm1_g[l][None, :], w_in[l].astype(BF16),
            jnp.tile(q_norm_g[l], N_Q_HEADS)[None, :], jnp.tile(k_norm_g[l], N_KV_HEADS)[None, :],
            conv_w[l], conv_b[l][None, :],
            _group_gate_weights(lru_w_r[l, 0], lru_w_i[l, 0]), _group_gate_weights(lru_w_r[l, 1], lru_w_i[l, 1]),
            bias, lru_lambda[l],
            w_out[l, :ATTN_W].astype(BF16), w_out[l, ATTN_W:].astype(BF16),
            norm2_g[l][None, :], router_w[l].T.astype(BF16),
            exp_w_gate[l], exp_w_up[l], exp_w_down[l],
        ))
    ys = []
    for x in (x_prompt, x_sample):
        for weights in layers:
            x = _layer(x, weights)
        ys.append(x)
    return tuple(ys)
```
